```python
import jax, jax.numpy as jnp
from jax import lax
import numpy as np

D_MODEL = 2048
BATCH = 4
SEQ = 8192
DEPTH = 1
DEC_BATCH = 1
DEC_SEQ = 16384
PAST_LEN = 128

GRID_W = 64
HEAD_DIM = 128
NA_HEADS = 8
NA_WIN_H = 8
NA_WIN_W = 16
GQA_Q_HEADS = 8
GQA_KV_HEADS = 2
ROPE_THETA = 10000.0
ROPE_AXIS_PAIRS = HEAD_DIM // 4
Q_BLOCK = 128
N_MEM = 256
CROSS_HEADS = 4
D_FF = 4 * D_MODEL
EPS = 1e-6
NEG_INF = -1e30
NA_W = NA_HEADS * HEAD_DIM
GQA_QW = GQA_Q_HEADS * HEAD_DIM
GQA_KVW = GQA_KV_HEADS * HEAD_DIM
CROSS_W = CROSS_HEADS * HEAD_DIM
IN_SPLITS = (NA_W, NA_W, NA_W, GQA_QW, GQA_KVW, GQA_KVW, D_MODEL, D_MODEL)
D_IN = sum(IN_SPLITS)

kernel_name = 'hybrid_natten_gqa_xattn_encoder'


def _rmsnorm(x, g):
    xf = x.astype(jnp.float32)
    y = xf * lax.rsqrt(jnp.mean(xf * xf, axis=-1, keepdims=True) + EPS)
    return (y * g.astype(jnp.float32)).astype(x.dtype)


def _neighbourhood_attention(q, k, v, rpb):
    b, s, h, dh = q.shape
    rows = s // GRID_W
    kh = min(NA_WIN_H, rows)
    qg = q.reshape(b, rows, GRID_W, h, dh).transpose(1, 0, 2, 3, 4)
    kg = k.reshape(b, rows, GRID_W, h, dh)
    vg = v.reshape(b, rows, GRID_W, h, dh)
    cols = np.arange(GRID_W)
    col_start = np.clip(cols - NA_WIN_W // 2, 0, GRID_W - NA_WIN_W)
    col_mask = (cols[None, :] >= col_start[:, None]) & (cols[None, :] < col_start[:, None] + NA_WIN_W)
    col_idx = np.clip(cols[None, :] - cols[:, None] + NA_WIN_W - 1, 0, 2 * NA_WIN_W - 2)
    band_mask = jnp.asarray(np.broadcast_to(col_mask[:, None, :], (GRID_W, kh, GRID_W)).reshape(GRID_W, kh * GRID_W))
    scale = dh ** -0.5

    def row_block(args):
        r, q_r = args
        start = jnp.clip(r - kh // 2, 0, rows - kh)
        k_r = lax.dynamic_slice_in_dim(kg, start, kh, axis=1).reshape(b, kh * GRID_W, h, dh)
        v_r = lax.dynamic_slice_in_dim(vg, start, kh, axis=1).reshape(b, kh * GRID_W, h, dh)
        ridx = start + jnp.arange(kh) - r + NA_WIN_H - 1
        bias = rpb[:, ridx][:, :, col_idx]
        bias = bias.transpose(0, 2, 1, 3).reshape(h, GRID_W, kh * GRID_W).astype(jnp.float32)
        sc = jnp.einsum('bqhd,bkhd->bhqk', q_r, k_r).astype(jnp.float32) * scale + bias
        sc = jnp.where(band_mask, sc, NEG_INF)
        p = jax.nn.softmax(sc, axis=-1).astype(v_r.dtype)
        return jnp.einsum('bhqk,bkhd->bqhd', p, v_r)

    out = lax.map(row_block, (jnp.arange(rows), qg))
    return out.transpose(1, 0, 2, 3, 4).reshape(b, s, h * dh)


def _axial_rope_tables(s):
    t = jnp.arange(s)
    row = (t // GRID_W).astype(jnp.float32)
    col = (t % GRID_W).astype(jnp.float32)
    inv_freq = ROPE_THETA ** (-jnp.arange(ROPE_AXIS_PAIRS, dtype=jnp.float32) / ROPE_AXIS_PAIRS)
    ang_r = (row[:, None] * inv_freq[None, :])[:, None, :]
    ang_c = (col[:, None] * inv_freq[None, :])[:, None, :]
    return jnp.cos(ang_r), jnp.sin(ang_r), jnp.cos(ang_c), jnp.sin(ang_c)


def _rope_axis(x, cos, sin):
    x1, x2 = jnp.split(x, 2, axis=-1)
    return jnp.concatenate([x1 * cos - x2 * sin, x2 * cos + x1 * sin], axis=-1)


def _axial_rope(x, tables):
    cos_r, sin_r, cos_c, sin_c = tables
    xf = x.astype(jnp.float32)
    xr, xc = jnp.split(xf, 2, axis=-1)
    y = jnp.concatenate([_rope_axis(xr, cos_r, sin_r), _rope_axis(xc, cos_c, sin_c)], axis=-1)
    return y.astype(x.dtype)


def _gqa_attention(q, k, v):
    b, s, hq, dh = q.shape
    hkv = k.shape[2]
    g = hq // hkv
    nb = s // Q_BLOCK
    scale = dh ** -0.5
    qb = q.reshape(b, nb, Q_BLOCK, hkv, g, dh).transpose(1, 0, 2, 3, 4, 5)

    def block(q_blk):
        sc = jnp.einsum('bqkgd,bskd->bkgqs', q_blk, k).astype(jnp.float32) * scale
        p = jax.nn.softmax(sc, axis=-1).astype(v.dtype)
        return jnp.einsum('bkgqs,bskd->bqkgd', p, v)

    out = lax.map(block, qb)
    return out.transpose(1, 0, 2, 3, 4, 5).reshape(b, s, hq * dh)


def _cross_attention(h, mem_h, w_cq, w_ckv, w_co):
    b, s, _ = h.shape
    m = mem_h.shape[1]
    q = (h @ w_cq).reshape(b, s, CROSS_HEADS, HEAD_DIM)
    k, v = jnp.split(mem_h @ w_ckv, 2, axis=-1)
    k = k.reshape(b, m, CROSS_HEADS, HEAD_DIM)
    v = v.reshape(b, m, CROSS_HEADS, HEAD_DIM)
    sc = jnp.einsum('bqhd,bmhd->bhqm', q, k).astype(jnp.float32) * (HEAD_DIM ** -0.5)
    p = jax.nn.softmax(sc, axis=-1).astype(v.dtype)
    o = jnp.einsum('bhqm,bmhd->bqhd', p, v).reshape(b, s, CROSS_W)
    return o @ w_co


def _trunk(x, mem, g_mix, w_in, rpb, g_q, g_k, w_pa, w_pb, w_o, g_cross, g_mem, w_cq, w_ckv, w_co,
           g_mlp, w_up, w_down, g_final):
    b, s, _ = x.shape
    tables = _axial_rope_tables(s)
    split_pts = [int(p) for p in np.cumsum(IN_SPLITS)[:-1]]
    for l in range(DEPTH):
        h = _rmsnorm(x, g_mix[l])
        z = h @ w_in[l]
        qa, ka, va, qb, kb, vb, ga, gb = jnp.split(z, split_pts, axis=-1)
        y_a = _neighbourhood_attention(qa.reshape(b, s, NA_HEADS, HEAD_DIM),
                                       ka.reshape(b, s, NA_HEADS, HEAD_DIM),
                                       va.reshape(b, s, NA_HEADS, HEAD_DIM), rpb[l]) @ w_pa[l]
        qb = _axial_rope(_rmsnorm(qb.reshape(b, s, GQA_Q_HEADS, HEAD_DIM), g_q[l]), tables)
        kb = _axial_rope(_rmsnorm(kb.reshape(b, s, GQA_KV_HEADS, HEAD_DIM), g_k[l]), tables)
        y_b = _gqa_attention(qb, kb, vb.reshape(b, s, GQA_KV_HEADS, HEAD_DIM)) @ w_pb[l]
        mixed = jax.nn.sigmoid(ga) * y_a + jax.nn.sigmoid(gb) * y_b
        x = x + mixed @ w_o[l]
        x = x + _cross_attention(_rmsnorm(x, g_cross[l]), _rmsnorm(mem, g_mem[l]), w_cq[l], w_ckv[l], w_co[l])
        hm = _rmsnorm(x, g_mlp[l]) @ w_up[l]
        x = x + jnp.square(jax.nn.relu(hm)) @ w_down[l]
    return _rmsnorm(x, g_final)


def setup_inputs(seed: int = 0) -> dict:
    key = jax.random.key(seed)
    ks = jax.random.split(key, 24)
    f32 = jnp.float32

    def w(k, shape, fan_in):
        return jax.random.normal(k, shape, f32) * (fan_in ** -0.5)

    def gain(k, shape):
        return 1.0 + 0.01 * jax.random.normal(k, shape, f32)

    return {
        'x_prompt': jax.random.normal(ks[0], (BATCH, SEQ, D_MODEL), f32),
        'x_sample': jax.random.normal(ks[1], (DEC_BATCH, DEC_SEQ, D_MODEL), f32),
        'mem_prompt': jax.random.normal(ks[2], (BATCH, N_MEM, D_MODEL), f32),
        'mem_sample': jax.random.normal(ks[3], (DEC_BATCH, N_MEM, D_MODEL), f32),
        'g_mix': gain(ks[4], (DEPTH, D_MODEL)),
        'w_in': w(ks[5], (DEPTH, D_MODEL, D_IN), D_MODEL),
        'rpb': 0.02 * jax.random.normal(ks[6], (DEPTH, NA_HEADS, 2 * NA_WIN_H - 1, 2 * NA_WIN_W - 1), f32),
        'g_q': gain(ks[7], (DEPTH, HEAD_DIM)),
        'g_k': gain(ks[8], (DEPTH, HEAD_DIM)),
        'w_pa': w(ks[9], (DEPTH, NA_W, D_MODEL), NA_W),
        'w_pb': w(ks[10], (DEPTH, GQA_QW, D_MODEL), GQA_QW),
        'w_o': w(ks[11], (DEPTH, D_MODEL, D_MODEL), D_MODEL),
        'g_cross': gain(ks[12], (DEPTH, D_MODEL)),
        'g_mem': gain(ks[13], (DEPTH, D_MODEL)),
        'w_cq': w(ks[14], (DEPTH, D_MODEL, CROSS_W), D_MODEL),
        'w_ckv': w(ks[15], (DEPTH, D_MODEL, 2 * CROSS_W), D_MODEL),
        'w_co': w(ks[16], (DEPTH, CROSS_W, D_MODEL), CROSS_W),
        'g_mlp': gain(ks[17], (DEPTH, D_MODEL)),
        'w_up': w(ks[18], (DEPTH, D_MODEL, D_FF), D_MODEL),
        'w_down': w(ks[19], (DEPTH, D_FF, D_MODEL), D_FF),
        'g_final': gain(ks[20], (D_MODEL,)),
    }


def reference(x_prompt, x_sample, mem_prompt, mem_sample, g_mix, w_in, rpb, g_q, g_k, w_pa, w_pb, w_o,
              g_cross, g_mem, w_cq, w_ckv, w_co, g_mlp, w_up, w_down, g_final):
    y_prompt = _trunk(x_prompt, mem_prompt, g_mix, w_in, rpb, g_q, g_k, w_pa, w_pb, w_o, g_cross, g_mem,
                      w_cq, w_ckv, w_co, g_mlp, w_up, w_down, g_final)
    y_sample = _trunk(x_sample, mem_sample, g_mix, w_in, rpb, g_q, g_k, w_pa, w_pb, w_o, g_cross, g_mem,
                      w_cq, w_ckv, w_co, g_mlp, w_up, w_down, g_final)
    return (y_prompt, y_sample)
```

```python
import functools

import jax
import jax.numpy as jnp
import numpy as np
from jax import lax
from jax.experimental import pallas as pl
from jax.experimental.pallas import tpu as pltpu

D_MODEL = 2048
GRID_W = 64
HEAD_DIM = 128
NA_HEADS = 8
NA_WIN_H = 8
NA_WIN_W = 16
GQA_Q_HEADS = 8
GQA_KV_HEADS = 2
GQA_GROUP = GQA_Q_HEADS // GQA_KV_HEADS
ROPE_THETA = 10000.0
ROPE_AXIS_PAIRS = HEAD_DIM // 4
N_MEM = 256
CROSS_HEADS = 4
D_FF = 4 * D_MODEL
EPS = 1e-6
NEG_INF = -1e30
NA_W = NA_HEADS * HEAD_DIM
GQA_QW = GQA_Q_HEADS * HEAD_DIM
GQA_KVW = GQA_KV_HEADS * HEAD_DIM
CROSS_W = CROSS_HEADS * HEAD_DIM
SCALE = HEAD_DIM ** -0.5

Z_GQ = 3 * NA_W
Z_GA = Z_GQ + GQA_QW
Z_GB = Z_GA + D_MODEL
Z_GK = Z_GB + D_MODEL
Z_GV = Z_GK + GQA_KVW
D_IN = Z_GV + GQA_KVW

NA_QROWS = 8
NA_KROWS = 16
NA_TQ = NA_QROWS * GRID_W
NA_TK = NA_KROWS * GRID_W

BF16 = jnp.bfloat16
F32 = jnp.float32
VMEM_LIMIT = 56 * 1024 * 1024


def _params(semantics):
    return pltpu.CompilerParams(dimension_semantics=semantics, vmem_limit_bytes=VMEM_LIMIT)


def _rms_scale(xf):
    return lax.rsqrt(jnp.mean(xf * xf, axis=-1, keepdims=True) + EPS)


def _norm_kernel(x_ref, g_ref, o_ref):
    xf = x_ref[...]
    o_ref[...] = (xf * _rms_scale(xf) * g_ref[...]).astype(BF16)


def _norm_bf16(x, g, tm):
    m, d = x.shape
    return pl.pallas_call(
        _norm_kernel,
        grid=(m // tm,),
        in_specs=[pl.BlockSpec((tm, d), lambda i: (i, 0)), pl.BlockSpec((1, d), lambda i: (0, 0))],
        out_specs=pl.BlockSpec((tm, d), lambda i: (i, 0)),
        out_shape=jax.ShapeDtypeStruct((m, d), BF16),
        compiler_params=_params(("parallel",)),
        name="norm_bf16",
    )(x, g)


def _matmul_kernel(a_ref, w_ref, o_ref):
    o_ref[...] = jnp.dot(a_ref[...], w_ref[...], preferred_element_type=F32).astype(o_ref.dtype)


def _matmul(a, w, tm, tn):
    m, k = a.shape
    n = w.shape[1]
    return pl.pallas_call(
        _matmul_kernel,
        grid=(n // tn, m // tm),
        in_specs=[pl.BlockSpec((tm, k), lambda j, i: (i, 0)), pl.BlockSpec((k, tn), lambda j, i: (0, j))],
        out_specs=pl.BlockSpec((tm, tn), lambda j, i: (i, j)),
        out_shape=jax.ShapeDtypeStruct((m, n), BF16),
        compiler_params=_params(("parallel", "parallel")),
        name="in_proj",
    )(a, w)


def _rope_tables(s):
    t = jnp.arange(s)
    row = (t // GRID_W).astype(F32)
    col = (t % GRID_W).astype(F32)
    inv_freq = ROPE_THETA ** (-jnp.arange(ROPE_AXIS_PAIRS, dtype=F32) / ROPE_AXIS_PAIRS)
    ang_r = row[:, None] * inv_freq[None, :]
    ang_c = col[:, None] * inv_freq[None, :]
    cr, sr, cc, sc = jnp.cos(ang_r), jnp.sin(ang_r), jnp.cos(ang_c), jnp.sin(ang_c)
    zero = jnp.zeros_like(sr)
    c = jnp.concatenate([cr, cr, cc, cc], axis=-1)
    s1 = jnp.concatenate([zero, sr, zero, sc], axis=-1)
    s2 = jnp.concatenate([-sr, zero, -sc, zero], axis=-1)
    return c, s1, s2


def _qk_prep_kernel(q_ref, k_ref, c_ref, s1_ref, s2_ref, gq_ref, gk_ref, qo_ref, ko_ref):
    c = c_ref[...]
    s1 = s1_ref[...]
    s2 = s2_ref[...]
    half = ROPE_AXIS_PAIRS

    def norm_rope(xh, g):
        xf = xh.astype(F32)
        y = xf * _rms_scale(xf) * g
        return y * c + pltpu.roll(y, half, 1) * s1 + pltpu.roll(y, HEAD_DIM - half, 1) * s2

    gq = gq_ref[...]
    gk = gk_ref[...]
    for h in range(GQA_Q_HEADS):
        sl = slice(h * HEAD_DIM, (h + 1) * HEAD_DIM)
        qo_ref[:, sl] = (norm_rope(q_ref[:, sl], gq) * SCALE).astype(BF16)
    for h in range(GQA_KV_HEADS):
        sl = slice(h * HEAD_DIM, (h + 1) * HEAD_DIM)
        ko_ref[:, sl] = norm_rope(k_ref[:, sl], gk).astype(BF16)


def _qk_prep(z, tables, g_q, g_k, s, tm):
    m = z.shape[0]
    nt = s // tm
    tab_spec = pl.BlockSpec((tm, HEAD_DIM), lambda i: (i % nt, 0))
    vec_spec = pl.BlockSpec((1, HEAD_DIM), lambda i: (0, 0))
    return pl.pallas_call(
        _qk_prep_kernel,
        grid=(m // tm,),
        in_specs=[
            pl.BlockSpec((tm, GQA_QW), lambda i: (i, Z_GQ // GQA_QW)),
            pl.BlockSpec((tm, GQA_KVW), lambda i: (i, Z_GK // GQA_KVW)),
            tab_spec, tab_spec, tab_spec, vec_spec, vec_spec,
        ],
        out_specs=[pl.BlockSpec((tm, GQA_QW), lambda i: (i, 0)), pl.BlockSpec((tm, GQA_KVW), lambda i: (i, 0))],
        out_shape=[jax.ShapeDtypeStruct((m, GQA_QW), BF16), jax.ShapeDtypeStruct((m, GQA_KVW), BF16)],
        compiler_params=_params(("parallel",)),
        name="qk_prep",
    )(z, z, *tables, g_q, g_k)


def _gqa_kernel(q_ref, k_ref, v_ref, o_ref, qs_ref, m_ref, l_ref, acc_ref, *, tq, tk, nk):
    for g in range(GQA_GROUP):
        qs_ref[g * tq:(g + 1) * tq, :] = q_ref[:, g * HEAD_DIM:(g + 1) * HEAD_DIM]
    m_ref[...] = jnp.full(m_ref.shape, -jnp.inf, F32)
    l_ref[...] = jnp.zeros(l_ref.shape, F32)
    acc_ref[...] = jnp.zeros(acc_ref.shape, F32)

    def body(c, carry):
        off = pl.multiple_of(c * tk, tk)
        k = k_ref[pl.ds(off, tk), :]
        v = v_ref[pl.ds(off, tk), :]
        s = lax.dot_general(qs_ref[...], k, (((1,), (1,)), ((), ())), preferred_element_type=F32)
        m_prev = m_ref[...]
        m_new = jnp.maximum(m_prev, jnp.max(s, axis=-1, keepdims=True))
        alpha = jnp.exp(m_prev - m_new)
        p = jnp.exp(s - m_new)
        l_ref[...] = alpha * l_ref[...] + jnp.sum(p, axis=-1, keepdims=True)
        acc_ref[...] = alpha * acc_ref[...] + jnp.dot(p.astype(BF16), v, preferred_element_type=F32)
        m_ref[...] = m_new
        return carry

    lax.fori_loop(0, nk, body, 0)
    out = acc_ref[...] / l_ref[...]
    for g in range(GQA_GROUP):
        o_ref[:, g * HEAD_DIM:(g + 1) * HEAD_DIM] = out[g * tq:(g + 1) * tq, :].astype(BF16)


def _gqa_attention(q, k, z, b, s, tq, tk):
    m = q.shape[0]
    nq = s // tq
    gw = GQA_GROUP * HEAD_DIM
    kernel = functools.partial(_gqa_kernel, tq=tq, tk=tk, nk=s // tk)
    return pl.pallas_call(
        kernel,
        grid=(b, GQA_KV_HEADS, nq),
        in_specs=[
            pl.BlockSpec((tq, gw), lambda bi, kh, i: (bi * nq + i, kh)),
            pl.BlockSpec((s, HEAD_DIM), lambda bi, kh, i: (bi, kh)),
            pl.BlockSpec((s, HEAD_DIM), lambda bi, kh, i: (bi, Z_GV // HEAD_DIM + kh)),
        ],
        out_specs=pl.BlockSpec((tq, gw), lambda bi, kh, i: (bi * nq + i, kh)),
        out_shape=jax.ShapeDtypeStruct((m, GQA_QW), BF16),
        scratch_shapes=[
            pltpu.VMEM((GQA_GROUP * tq, HEAD_DIM), BF16),
            pltpu.VMEM((GQA_GROUP * tq, 1), F32),
            pltpu.VMEM((GQA_GROUP * tq, 1), F32),
            pltpu.VMEM((GQA_GROUP * tq, HEAD_DIM), F32),
        ],
        compiler_params=_params(("parallel", "parallel", "parallel")),
        name="gqa_flash",
    )(q, k, z)


def _na_tables(rpb):
    rows = 4 * NA_QROWS
    qi = np.arange(NA_TQ)
    kj = np.arange(NA_TK)
    ridx_l, cidx_l, mask_l = [], [], []
    for r0 in (0, NA_QROWS, rows - NA_QROWS):
        kb = int(np.clip(r0 - NA_WIN_H // 2, 0, rows - NA_KROWS))
        qr = r0 + qi // GRID_W
        qc = qi % GRID_W
        kr = kb + kj // GRID_W
        kc = kj % GRID_W
        start = np.clip(qr - NA_WIN_H // 2, 0, rows - NA_WIN_H)
        row_ok = (kr[None, :] >= start[:, None]) & (kr[None, :] < start[:, None] + NA_WIN_H)
        col_start = np.clip(qc - NA_WIN_W // 2, 0, GRID_W - NA_WIN_W)
        col_ok = (kc[None, :] >= col_start[:, None]) & (kc[None, :] < col_start[:, None] + NA_WIN_W)
        ridx_l.append(np.clip(kr[None, :] - qr[:, None] + NA_WIN_H - 1, 0, 2 * NA_WIN_H - 2))
        cidx_l.append(np.clip(kc[None, :] - qc[:, None] + NA_WIN_W - 1, 0, 2 * NA_WIN_W - 2))
        mask_l.append(row_ok & col_ok)
    ridx = np.stack(ridx_l)
    cidx = np.stack(cidx_l)
    mask = np.stack(mask_l)
    bias = rpb[:, ridx, cidx].astype(F32)
    bias = jnp.where(mask[None], bias, 0.0)
    return bias, jnp.asarray(mask.astype(np.float32))


def _na_kernel(q_ref, k_ref, v_ref, bias_ref, mask_ref, o_ref, *, rows):
    rb = pl.program_id(2)
    kb = jnp.clip(rb * NA_QROWS - NA_WIN_H // 2, 0, rows - NA_KROWS)
    off = pl.multiple_of(kb * GRID_W, (NA_WIN_H // 2) * GRID_W)
    k = k_ref[pl.ds(off, NA_TK), :]
    v = v_ref[pl.ds(off, NA_TK), :]
    s = lax.dot_general(q_ref[...], k, (((1,), (1,)), ((), ())), preferred_element_type=F32)
    s = jnp.where(mask_ref[...] != 0.0, s * SCALE + bias_ref[...], NEG_INF)
    m = jnp.max(s, axis=-1, keepdims=True)
    p = jnp.exp(s - m)
    l = jnp.sum(p, axis=-1, keepdims=True)
    o = jnp.dot(p.astype(BF16), v, preferred_element_type=F32)
    o_ref[...] = (o / l).astype(BF16)


def _na_attention(z, bias, mask, b, s):
    m = z.shape[0]
    rows = s // GRID_W
    nrb = rows // NA_QROWS

    def btype(rb):
        return jnp.where(rb == 0, 0, jnp.where(rb == nrb - 1, 2, 1))

    kernel = functools.partial(_na_kernel, rows=rows)
    return pl.pallas_call(
        kernel,
        grid=(NA_HEADS, b, nrb),
        in_specs=[
            pl.BlockSpec((NA_TQ, HEAD_DIM), lambda h, bi, rb: (bi * nrb + rb, h)),
            pl.BlockSpec((s, HEAD_DIM), lambda h, bi, rb: (bi, NA_HEADS + h)),
            pl.BlockSpec((s, HEAD_DIM), lambda h, bi, rb: (bi, 2 * NA_HEADS + h)),
            pl.BlockSpec((None, None, NA_TQ, NA_TK), lambda h, bi, rb: (h, btype(rb), 0, 0)),
            pl.BlockSpec((None, NA_TQ, NA_TK), lambda h, bi, rb: (btype(rb), 0, 0)),
        ],
        out_specs=pl.BlockSpec((NA_TQ, HEAD_DIM), lambda h, bi, rb: (bi * nrb + rb, h)),
        out_shape=jax.ShapeDtypeStruct((m, NA_W), BF16),
        compiler_params=_params(("parallel", "parallel", "parallel")),
        name="na_attn",
    )(z, z, z, bias, mask)


def _mix_kernel(x_ref, ya_ref, yb_ref, ga_ref, gb_ref, wpa_ref, wpb_ref, wo_ref, o_ref):
    pa = jnp.dot(ya_ref[...], wpa_ref[...], preferred_element_type=F32)
    pb = jnp.dot(yb_ref[...], wpb_ref[...], preferred_element_type=F32)
    mixed = jax.nn.sigmoid(ga_ref[...].astype(F32)) * pa + jax.nn.sigmoid(gb_ref[...].astype(F32)) * pb
    o_ref[...] = x_ref[...] + jnp.dot(mixed.astype(BF16), wo_ref[...], preferred_element_type=F32)


def _mix(x, ya, yb, z, w_pa, w_pb, w_o, tm):
    m, d = x.shape
    row = lambda i: (i, 0)
    const = lambda i: (0, 0)
    resident = dict(pipeline_mode=pl.Buffered(1))
    return pl.pallas_call(
        _mix_kernel,
        grid=(m // tm,),
        in_specs=[
            pl.BlockSpec((tm, d), row),
            pl.BlockSpec((tm, NA_W), row),
            pl.BlockSpec((tm, GQA_QW), row),
            pl.BlockSpec((tm, d), lambda i: (i, Z_GA // D_MODEL)),
            pl.BlockSpec((tm, d), lambda i: (i, Z_GB // D_MODEL)),
            pl.BlockSpec((NA_W, d), const, **resident),
            pl.BlockSpec((GQA_QW, d), const, **resident),
            pl.BlockSpec((d, d), const, **resident),
        ],
        out_specs=pl.BlockSpec((tm, d), row),
        out_shape=jax.ShapeDtypeStruct((m, d), F32),
        compiler_params=_params(("parallel",)),
        name="mix_out_proj",
    )(x, ya, yb, z, z, w_pa, w_pb, w_o)


def _mem_kv_kernel(mem_ref, g_ref, w_ref, o_ref):
    xf = mem_ref[...]
    h = (xf * _rms_scale(xf) * g_ref[...]).astype(BF16)
    o_ref[...] = jnp.dot(h, w_ref[...], preferred_element_type=F32).astype(BF16)


def _mem_kv(mem, g_mem, w_ckv):
    m, d = mem.shape
    n = w_ckv.shape[1]
    return pl.pallas_call(
        _mem_kv_kernel,
        grid=(m // N_MEM,),
        in_specs=[
            pl.BlockSpec((N_MEM, d), lambda i: (i, 0)),
            pl.BlockSpec((1, d), lambda i: (0, 0)),
            pl.BlockSpec((d, n), lambda i: (0, 0)),
        ],
        out_specs=pl.BlockSpec((N_MEM, n), lambda i: (i, 0)),
        out_shape=jax.ShapeDtypeStruct((m, n), BF16),
        compiler_params=_params(("parallel",)),
        name="mem_kv",
    )(mem, g_mem, w_ckv)


def _cross_kernel(x_ref, kv_ref, g_ref, wq_ref, wo_ref, o_ref):
    xf = x_ref[...]
    h = (xf * _rms_scale(xf) * g_ref[...]).astype(BF16)
    q = (jnp.dot(h, wq_ref[...], preferred_element_type=F32) * SCALE).astype(BF16)
    outs = []
    for hd in range(CROSS_HEADS):
        sl = slice(hd * HEAD_DIM, (hd + 1) * HEAD_DIM)
        k = kv_ref[:, sl]
        v = kv_ref[:, CROSS_W + hd * HEAD_DIM:CROSS_W + (hd + 1) * HEAD_DIM]
        s = lax.dot_general(q[:, sl], k, (((1,), (1,)), ((), ())), preferred_element_type=F32)
        p = jnp.exp(s - jnp.max(s, axis=-1, keepdims=True))
        l = jnp.sum(p, axis=-1, keepdims=True)
        outs.append((jnp.dot(p.astype(BF16), v, preferred_element_type=F32) / l).astype(BF16))
    o = jnp.concatenate(outs, axis=-1)
    o_ref[...] = xf + jnp.dot(o, wo_ref[...], preferred_element_type=F32)


def _cross(x, kv, g_cross, w_cq, w_co, s, tm):
    m, d = x.shape
    nt = s // tm
    const = lambda i: (0, 0)
    return pl.pallas_call(
        _cross_kernel,
        grid=(m // tm,),
        in_specs=[
            pl.BlockSpec((tm, d), lambda i: (i, 0)),
            pl.BlockSpec((N_MEM, 2 * CROSS_W), lambda i: (i // nt, 0)),
            pl.BlockSpec((1, d), const),
            pl.BlockSpec((d, CROSS_W), const),
            pl.BlockSpec((CROSS_W, d), const),
        ],
        out_specs=pl.BlockSpec((tm, d), lambda i: (i, 0)),
        out_shape=jax.ShapeDtypeStruct((m, d), F32),
        compiler_params=_params(("parallel",)),
        name="cross_attn",
    )(x, kv, g_cross, w_cq, w_co)


def _mlp_kernel(x_ref, g_ref, wu_ref, wd_ref, gf_ref, o_ref, h_ref, *, nf):
    f = pl.program_id(1)

    @pl.when(f == 0)
    def _():
        xf = x_ref[...]
        h_ref[...] = (xf * _rms_scale(xf) * g_ref[...]).astype(BF16)
        o_ref[...] = xf

    u = jnp.dot(h_ref[...], wu_ref[...], preferred_element_type=F32)
    a = jnp.square(jnp.maximum(u, 0.0)).astype(BF16)
    o_ref[...] += jnp.dot(a, wd_ref[...], preferred_element_type=F32)

    @pl.when(f == nf - 1)
    def _():
        y = o_ref[...]
        o_ref[...] = y * _rms_scale(y) * gf_ref[...]


def _mlp(x, g_mlp, w_up, w_down, g_final, tm, tf):
    m, d = x.shape
    ff = w_up.shape[1]
    nf = ff // tf
    kernel = functools.partial(_mlp_kernel, nf=nf)
    return pl.pallas_call(
        kernel,
        grid=(m // tm, nf),
        in_specs=[
            pl.BlockSpec((tm, d), lambda i, f: (i, 0)),
            pl.BlockSpec((1, d), lambda i, f: (0, 0)),
            pl.BlockSpec((d, tf), lambda i, f: (0, f)),
            pl.BlockSpec((tf, d), lambda i, f: (f, 0)),
            pl.BlockSpec((1, d), lambda i, f: (0, 0)),
        ],
        out_specs=pl.BlockSpec((tm, d), lambda i, f: (i, 0)),
        out_shape=jax.ShapeDtypeStruct((m, d), F32),
        scratch_shapes=[pltpu.VMEM((tm, d), BF16)],
        compiler_params=_params(("parallel", "arbitrary")),
        name="mlp_final_norm",
    )(x, g_mlp, w_up, w_down, g_final)


def _trunk(x, mem, w, *, tiles):
    b, s, d = x.shape
    m = b * s
    x2 = x.reshape(m, d)
    h = _norm_bf16(x2, w["g_mix"], tiles["norm"])
    z = _matmul(h, w["w_in"], tiles["in_m"], tiles["in_n"])
    q_rope, k_rope = _qk_prep(z, _rope_tables(s), w["g_q"], w["g_k"], s, tiles["prep"])
    ya = _na_attention(z, w["na_bias"], w["na_mask"], b, s)
    yb = _gqa_attention(q_rope, k_rope, z, b, s, tiles["gqa_q"], tiles["gqa_k"])
    x2 = _mix(x2, ya, yb, z, w["w_pa"], w["w_pb"], w["w_o"], tiles["mix"])
    kv = _mem_kv(mem.reshape(b * N_MEM, d), w["g_mem"], w["w_ckv"])
    x2 = _cross(x2, kv, w["g_cross"], w["w_cq"], w["w_co"], s, tiles["cross"])
    y = _mlp(x2, w["g_mlp"], w["w_up"], w["w_down"], w["g_final"], tiles["mlp_m"], tiles["mlp_f"])
    return y.reshape(b, s, d)


def _prepare_weights(g_mix, w_in, rpb, g_q, g_k, w_pa, w_pb, w_o, g_cross, g_mem, w_cq, w_ckv, w_co,
                     g_mlp, w_up, w_down, g_final):
    split = 3 * NA_W + GQA_QW
    kv_end = split + 2 * GQA_KVW
    w_in0 = w_in[0]
    w_in_p = jnp.concatenate([w_in0[:, :split], w_in0[:, kv_end:], w_in0[:, split:kv_end]], axis=1)
    na_bias, na_mask = _na_tables(rpb[0])
    return dict(
        g_mix=g_mix[0][None], w_in=w_in_p.astype(BF16), na_bias=na_bias, na_mask=na_mask,
        g_q=g_q[0][None], g_k=g_k[0][None],
        w_pa=w_pa[0].astype(BF16), w_pb=w_pb[0].astype(BF16), w_o=w_o[0].astype(BF16),
        g_cross=g_cross[0][None], g_mem=g_mem[0][None],
        w_cq=w_cq[0].astype(BF16), w_ckv=w_ckv[0].astype(BF16), w_co=w_co[0].astype(BF16),
        g_mlp=g_mlp[0][None], w_up=w_up[0].astype(BF16), w_down=w_down[0].astype(BF16),
        g_final=g_final[None],
    )


TILES = dict(norm=512, in_m=1024, in_n=2176, prep=512, gqa_q=256, gqa_k=512, mix=512, cross=512,
             mlp_m=512, mlp_f=512)


def kernel(x_prompt, x_sample, mem_prompt, mem_sample, g_mix, w_in, rpb, g_q, g_k, w_pa, w_pb, w_o, g_cross, g_mem, w_cq, w_ckv, w_co, g_mlp, w_up, w_down, g_final):
    w = _prepare_weights(g_mix, w_in, rpb, g_q, g_k, w_pa, w_pb, w_o, g_cross, g_mem, w_cq, w_ckv, w_co,
                         g_mlp, w_up, w_down, g_final)
    y_prompt = _trunk(x_prompt, mem_prompt, w, tiles=TILES)
    y_sample = _trunk(x_sample, mem_sample, w, tiles=TILES)
    return (y_prompt, y_sample)
```

```python
import functools

import jax
import jax.numpy as jnp
import numpy as np
from jax import lax
from jax.experimental import pallas as pl
from jax.experimental.pallas import tpu as pltpu

D_MODEL = 2048
GRID_W = 64
HEAD_DIM = 128
NA_HEADS = 8
NA_WIN_H = 8
NA_WIN_W = 16
GQA_Q_HEADS = 8
GQA_KV_HEADS = 2
GQA_GROUP = GQA_Q_HEADS // GQA_KV_HEADS
ROPE_THETA = 10000.0
ROPE_AXIS_PAIRS = HEAD_DIM // 4
N_MEM = 256
CROSS_HEADS = 4
D_FF = 4 * D_MODEL
EPS = 1e-6
NEG_INF = -1e30
NA_W = NA_HEADS * HEAD_DIM
GQA_QW = GQA_Q_HEADS * HEAD_DIM
GQA_KVW = GQA_KV_HEADS * HEAD_DIM
CROSS_W = CROSS_HEADS * HEAD_DIM
SCALE = HEAD_DIM ** -0.5
LOG2E = 1.4426950408889634

Z_GQ = 3 * NA_W
Z_GA = Z_GQ + GQA_QW
Z_GB = Z_GA + D_MODEL
Z_GK = Z_GB + D_MODEL
Z_GV = Z_GK + GQA_KVW
D_IN = Z_GV + GQA_KVW

NA_QROWS = 8
NA_KROWS = 16
NA_TQ = NA_QROWS * GRID_W
NA_TK = NA_KROWS * GRID_W

BF16 = jnp.bfloat16
F32 = jnp.float32
VMEM_LIMIT = 56 * 1024 * 1024


def _params(semantics):
    return pltpu.CompilerParams(dimension_semantics=semantics, vmem_limit_bytes=VMEM_LIMIT)


def _rms_scale(xf):
    return lax.rsqrt(jnp.mean(xf * xf, axis=-1, keepdims=True) + EPS)


def _norm_kernel(x_ref, g_ref, o_ref):
    xf = x_ref[...]
    o_ref[...] = (xf * _rms_scale(xf) * g_ref[...]).astype(BF16)


def _norm_bf16(x, g, tm):
    m, d = x.shape
    return pl.pallas_call(
        _norm_kernel,
        grid=(m // tm,),
        in_specs=[pl.BlockSpec((tm, d), lambda i: (i, 0)), pl.BlockSpec((1, d), lambda i: (0, 0))],
        out_specs=pl.BlockSpec((tm, d), lambda i: (i, 0)),
        out_shape=jax.ShapeDtypeStruct((m, d), BF16),
        compiler_params=_params(("parallel",)),
        name="norm_bf16",
    )(x, g)


def _matmul_kernel(a_ref, w_ref, o_ref):
    o_ref[...] = jnp.dot(a_ref[...], w_ref[...], preferred_element_type=F32).astype(o_ref.dtype)


def _matmul(a, w, tm, tn):
    m, k = a.shape
    n = w.shape[1]
    return pl.pallas_call(
        _matmul_kernel,
        grid=(n // tn, m // tm),
        in_specs=[pl.BlockSpec((tm, k), lambda j, i: (i, 0)), pl.BlockSpec((k, tn), lambda j, i: (0, j))],
        out_specs=pl.BlockSpec((tm, tn), lambda j, i: (i, j)),
        out_shape=jax.ShapeDtypeStruct((m, n), BF16),
        compiler_params=_params(("parallel", "parallel")),
        name="in_proj",
    )(a, w)


def _rope_tables(s):
    t = jnp.arange(s)
    row = (t // GRID_W).astype(F32)
    col = (t % GRID_W).astype(F32)
    inv_freq = ROPE_THETA ** (-jnp.arange(ROPE_AXIS_PAIRS, dtype=F32) / ROPE_AXIS_PAIRS)
    ang_r = row[:, None] * inv_freq[None, :]
    ang_c = col[:, None] * inv_freq[None, :]
    cr, sr, cc, sc = jnp.cos(ang_r), jnp.sin(ang_r), jnp.cos(ang_c), jnp.sin(ang_c)
    zero = jnp.zeros_like(sr)
    c = jnp.concatenate([cr, cr, cc, cc], axis=-1)
    s1 = jnp.concatenate([zero, sr, zero, sc], axis=-1)
    s2 = jnp.concatenate([-sr, zero, -sc, zero], axis=-1)
    return c, s1, s2


def _qk_prep_kernel(q_ref, k_ref, v_ref, c_ref, s1_ref, s2_ref, gq_ref, gk_ref, qo_ref, ko_ref, vo_ref):
    c = c_ref[...]
    s1 = s1_ref[...]
    s2 = s2_ref[...]
    half = ROPE_AXIS_PAIRS

    def norm_rope(xh, g):
        xf = xh.astype(F32)
        y = xf * _rms_scale(xf) * g
        return y * c + pltpu.roll(y, half, 1) * s1 + pltpu.roll(y, HEAD_DIM - half, 1) * s2

    gq = gq_ref[...]
    gk = gk_ref[...]
    for h in range(GQA_Q_HEADS):
        sl = slice(h * HEAD_DIM, (h + 1) * HEAD_DIM)
        qo_ref[:, sl] = (norm_rope(q_ref[:, sl], gq) * (SCALE * LOG2E)).astype(BF16)
    ones = jnp.ones((HEAD_DIM, vo_ref.shape[-1]), BF16)
    for h in range(GQA_KV_HEADS):
        sl = slice(h * HEAD_DIM, (h + 1) * HEAD_DIM)
        ko_ref[:, sl] = norm_rope(k_ref[:, sl], gk).astype(BF16)
        vo_ref[h, :HEAD_DIM, :] = v_ref[:, sl].astype(F32).T.astype(BF16)
        vo_ref[h, HEAD_DIM:, :] = ones


def _qk_prep(z, tables, g_q, g_k, s, tm):
    m = z.shape[0]
    nt = s // tm
    tab_spec = pl.BlockSpec((tm, HEAD_DIM), lambda i: (i % nt, 0))
    vec_spec = pl.BlockSpec((1, HEAD_DIM), lambda i: (0, 0))
    row = lambda i: (i, 0)
    return pl.pallas_call(
        _qk_prep_kernel,
        grid=(m // tm,),
        in_specs=[
            pl.BlockSpec((tm, GQA_QW), lambda i: (i, Z_GQ // GQA_QW)),
            pl.BlockSpec((tm, GQA_KVW), lambda i: (i, Z_GK // GQA_KVW)),
            pl.BlockSpec((tm, GQA_KVW), lambda i: (i, Z_GV // GQA_KVW)),
            tab_spec, tab_spec, tab_spec, vec_spec, vec_spec,
        ],
        out_specs=[pl.BlockSpec((tm, GQA_QW), row), pl.BlockSpec((tm, GQA_KVW), row),
                   pl.BlockSpec((None, GQA_KV_HEADS, None, 2 * HEAD_DIM, tm), lambda i: (i // nt, 0, i % nt, 0, 0))],
        out_shape=[jax.ShapeDtypeStruct((m, GQA_QW), BF16), jax.ShapeDtypeStruct((m, GQA_KVW), BF16),
                   jax.ShapeDtypeStruct((m // s, GQA_KV_HEADS, nt, 2 * HEAD_DIM, tm), BF16)],
        compiler_params=_params(("parallel",)),
        name="qk_prep",
    )(z, z, z, *tables, g_q, g_k)


def _gqa_kernel(q_ref, k_ref, vt_ref, o_ref, s_ref, m_ref, acc_ref, *, tk, nk):
    m_ref[...] = jnp.full(m_ref.shape, -jnp.inf, F32)
    acc_ref[...] = jnp.zeros(acc_ref.shape, F32)

    def scores(c, slot):
        off = pl.multiple_of(c * tk, tk)
        k = k_ref[pl.ds(off, tk), :]
        for g in range(GQA_GROUP):
            q = q_ref[:, g * HEAD_DIM:(g + 1) * HEAD_DIM]
            s_ref[slot, g] = lax.dot_general(k, q, (((1,), (1,)), ((), ())), preferred_element_type=F32)

    def update(c, slot):
        vt = vt_ref[c]
        for g in range(GQA_GROUP):
            st = s_ref[slot, g]
            m_prev = m_ref[g]
            m_new = jnp.maximum(m_prev, jnp.max(st, axis=0, keepdims=True))
            alpha = jnp.exp2(m_prev - m_new)
            pt = jnp.exp2(st - m_new).astype(BF16)
            acc_ref[g] = acc_ref[g] * alpha + jnp.dot(vt, pt, preferred_element_type=F32)
            m_ref[g] = m_new

    scores(0, 0)

    def body(i, carry):
        c = 2 * i
        scores(c + 1, 1)
        update(c, 0)
        scores(jnp.minimum(c + 2, nk - 1), 0)
        update(c + 1, 1)
        return carry

    lax.fori_loop(0, nk // 2, body, 0)
    for g in range(GQA_GROUP):
        acc = acc_ref[g]
        out_t = acc[:HEAD_DIM, :] / acc[HEAD_DIM:, :]
        o_ref[:, g * HEAD_DIM:(g + 1) * HEAD_DIM] = out_t.T.astype(BF16)


def _gqa_attention(q, k, vt_ext, b, s, tq, tk):
    m = q.shape[0]
    nq = s // tq
    nk = s // tk
    gw = GQA_GROUP * HEAD_DIM
    kernel = functools.partial(_gqa_kernel, tk=tk, nk=nk)
    return pl.pallas_call(
        kernel,
        grid=(b, GQA_KV_HEADS, nq),
        in_specs=[
            pl.BlockSpec((tq, gw), lambda bi, kh, i: (bi * nq + i, kh)),
            pl.BlockSpec((s, HEAD_DIM), lambda bi, kh, i: (bi, kh)),
            pl.BlockSpec((None, None, nk, 2 * HEAD_DIM, tk), lambda bi, kh, i: (bi, kh, 0, 0, 0)),
        ],
        out_specs=pl.BlockSpec((tq, gw), lambda bi, kh, i: (bi * nq + i, kh)),
        out_shape=jax.ShapeDtypeStruct((m, GQA_QW), BF16),
        scratch_shapes=[
            pltpu.VMEM((2, GQA_GROUP, tk, tq), F32),
            pltpu.VMEM((GQA_GROUP, 1, tq), F32),
            pltpu.VMEM((GQA_GROUP, 2 * HEAD_DIM, tq), F32),
        ],
        compiler_params=_params(("parallel", "parallel", "parallel")),
        name="gqa_flash",
    )(q, k, vt_ext)


def _na_tables(rpb):
    rows = 4 * NA_QROWS
    nr, nc = 2 * NA_WIN_H - 1, 2 * NA_WIN_W - 1
    cols = np.arange(GRID_W)
    cidx = np.clip(cols[None, :] - cols[:, None] + NA_WIN_W - 1, 0, nc - 1)
    col_start = np.clip(cols - NA_WIN_W // 2, 0, GRID_W - NA_WIN_W)
    col_ok = (cols[None, :] >= col_start[:, None]) & (cols[None, :] < col_start[:, None] + NA_WIN_W)
    sel_c = ((cidx[None] == np.arange(nc)[:, None, None]) & col_ok[None]).astype(np.float32)
    sel_r = np.zeros((3, NA_QROWS, NA_KROWS, nr), np.float32)
    for t, r0 in enumerate((0, NA_QROWS, rows - NA_QROWS)):
        kb = int(np.clip(r0 - NA_WIN_H // 2, 0, rows - NA_KROWS))
        for i in range(NA_QROWS):
            qr = r0 + i
            start = int(np.clip(qr - NA_WIN_H // 2, 0, rows - NA_WIN_H))
            for j in range(NA_KROWS):
                kr = kb + j
                if start <= kr < start + NA_WIN_H:
                    sel_r[t, i, j, kr - qr + NA_WIN_H - 1] = 1.0
    hp = lax.Precision.HIGHEST
    band = jnp.einsum("hrc,cqk->hrqk", rpb.astype(F32), jnp.asarray(sel_c), precision=hp)
    bias = jnp.einsum("tijr,hrqk->htiqjk", jnp.asarray(sel_r), band, precision=hp)
    bias = bias.reshape(NA_HEADS, 3, NA_TQ, NA_TK)
    mask = np.einsum("tijr,qk->tiqjk", sel_r, col_ok.astype(np.float32)).reshape(3, NA_TQ, NA_TK)
    return bias, jnp.asarray(mask)


def _na_kernel(q_ref, k_ref, v_ref, bias_ref, mask_ref, o_ref, *, rows):
    rb = pl.program_id(2)
    kb = jnp.clip(rb * NA_QROWS - NA_WIN_H // 2, 0, rows - NA_KROWS)
    off = pl.multiple_of(kb * GRID_W, (NA_WIN_H // 2) * GRID_W)
    k = k_ref[pl.ds(off, NA_TK), :]
    v = v_ref[pl.ds(off, NA_TK), :]
    s = lax.dot_general(q_ref[...], k, (((1,), (1,)), ((), ())), preferred_element_type=F32)
    s = jnp.where(mask_ref[...] != 0.0, s * SCALE + bias_ref[...], NEG_INF)
    m = jnp.max(s, axis=-1, keepdims=True)
    p = jnp.exp(s - m)
    l = jnp.sum(p, axis=-1, keepdims=True)
    o = jnp.dot(p.astype(BF16), v, preferred_element_type=F32)
    o_ref[...] = (o / l).astype(BF16)


def _na_attention(z, bias, mask, b, s):
    m = z.shape[0]
    rows = s // GRID_W
    nrb = rows // NA_QROWS

    def btype(rb):
        return jnp.where(rb == 0, 0, jnp.where(rb == nrb - 1, 2, 1))

    kernel = functools.partial(_na_kernel, rows=rows)
    return pl.pallas_call(
        kernel,
        grid=(NA_HEADS, b, nrb),
        in_specs=[
            pl.BlockSpec((NA_TQ, HEAD_DIM), lambda h, bi, rb: (bi * nrb + rb, h)),
            pl.BlockSpec((s, HEAD_DIM), lambda h, bi, rb: (bi, NA_HEADS + h)),
            pl.BlockSpec((s, HEAD_DIM), lambda h, bi, rb: (bi, 2 * NA_HEADS + h)),
            pl.BlockSpec((None, None, NA_TQ, NA_TK), lambda h, bi, rb: (h, btype(rb), 0, 0)),
            pl.BlockSpec((None, NA_TQ, NA_TK), lambda h, bi, rb: (btype(rb), 0, 0)),
        ],
        out_specs=pl.BlockSpec((NA_TQ, HEAD_DIM), lambda h, bi, rb: (bi * nrb + rb, h)),
        out_shape=jax.ShapeDtypeStruct((m, NA_W), BF16),
        compiler_params=_params(("parallel", "parallel", "parallel")),
        name="na_attn",
    )(z, z, z, bias, mask)


def _mix_kernel(x_ref, ya_ref, yb_ref, ga_ref, gb_ref, wpa_ref, wpb_ref, wo_ref, o_ref):
    pa = jnp.dot(ya_ref[...], wpa_ref[...], preferred_element_type=F32)
    pb = jnp.dot(yb_ref[...], wpb_ref[...], preferred_element_type=F32)
    mixed = jax.nn.sigmoid(ga_ref[...].astype(F32)) * pa + jax.nn.sigmoid(gb_ref[...].astype(F32)) * pb
    o_ref[...] = x_ref[...] + jnp.dot(mixed.astype(BF16), wo_ref[...], preferred_element_type=F32)


def _mix(x, ya, yb, z, w_pa, w_pb, w_o, tm):
    m, d = x.shape
    row = lambda i: (i, 0)
    const = lambda i: (0, 0)
    resident = dict(pipeline_mode=pl.Buffered(1))
    return pl.pallas_call(
        _mix_kernel,
        grid=(m // tm,),
        in_specs=[
            pl.BlockSpec((tm, d), row),
            pl.BlockSpec((tm, NA_W), row),
            pl.BlockSpec((tm, GQA_QW), row),
            pl.BlockSpec((tm, d), lambda i: (i, Z_GA // D_MODEL)),
            pl.BlockSpec((tm, d), lambda i: (i, Z_GB // D_MODEL)),
            pl.BlockSpec((NA_W, d), const, **resident),
            pl.BlockSpec((GQA_QW, d), const, **resident),
            pl.BlockSpec((d, d), const, **resident),
        ],
        out_specs=pl.BlockSpec((tm, d), row),
        out_shape=jax.ShapeDtypeStruct((m, d), F32),
        compiler_params=_params(("parallel",)),
        name="mix_out_proj",
    )(x, ya, yb, z, z, w_pa, w_pb, w_o)


def _mem_kv_kernel(mem_ref, g_ref, w_ref, o_ref):
    xf = mem_ref[...]
    h = (xf * _rms_scale(xf) * g_ref[...]).astype(BF16)
    o_ref[...] = jnp.dot(h, w_ref[...], preferred_element_type=F32).astype(BF16)


def _mem_kv(mem, g_mem, w_ckv):
    m, d = mem.shape
    n = w_ckv.shape[1]
    return pl.pallas_call(
        _mem_kv_kernel,
        grid=(m // N_MEM,),
        in_specs=[
            pl.BlockSpec((N_MEM, d), lambda i: (i, 0)),
            pl.BlockSpec((1, d), lambda i: (0, 0)),
            pl.BlockSpec((d, n), lambda i: (0, 0)),
        ],
        out_specs=pl.BlockSpec((N_MEM, n), lambda i: (i, 0)),
        out_shape=jax.ShapeDtypeStruct((m, n), BF16),
        compiler_params=_params(("parallel",)),
        name="mem_kv",
    )(mem, g_mem, w_ckv)


def _cross_kernel(x_ref, kv_ref, g_ref, wq_ref, wo_ref, o_ref):
    xf = x_ref[...]
    h = (xf * _rms_scale(xf) * g_ref[...]).astype(BF16)
    q = (jnp.dot(h, wq_ref[...], preferred_element_type=F32) * SCALE).astype(BF16)
    outs = []
    for hd in range(CROSS_HEADS):
        sl = slice(hd * HEAD_DIM, (hd + 1) * HEAD_DIM)
        k = kv_ref[:, sl]
        v = kv_ref[:, CROSS_W + hd * HEAD_DIM:CROSS_W + (hd + 1) * HEAD_DIM]
        s = lax.dot_general(q[:, sl], k, (((1,), (1,)), ((), ())), preferred_element_type=F32)
        p = jnp.exp(s - jnp.max(s, axis=-1, keepdims=True))
        l = jnp.sum(p, axis=-1, keepdims=True)
        outs.append((jnp.dot(p.astype(BF16), v, preferred_element_type=F32) / l).astype(BF16))
    o = jnp.concatenate(outs, axis=-1)
    o_ref[...] = xf + jnp.dot(o, wo_ref[...], preferred_element_type=F32)


def _cross(x, kv, g_cross, w_cq, w_co, s, tm):
    m, d = x.shape
    nt = s // tm
    const = lambda i: (0, 0)
    return pl.pallas_call(
        _cross_kernel,
        grid=(m // tm,),
        in_specs=[
            pl.BlockSpec((tm, d), lambda i: (i, 0)),
            pl.BlockSpec((N_MEM, 2 * CROSS_W), lambda i: (i // nt, 0)),
            pl.BlockSpec((1, d), const),
            pl.BlockSpec((d, CROSS_W), const),
            pl.BlockSpec((CROSS_W, d), const),
        ],
        out_specs=pl.BlockSpec((tm, d), lambda i: (i, 0)),
        out_shape=jax.ShapeDtypeStruct((m, d), F32),
        compiler_params=_params(("parallel",)),
        name="cross_attn",
    )(x, kv, g_cross, w_cq, w_co)


def _mlp_kernel(x_ref, g_ref, wu_ref, wd_ref, gf_ref, o_ref, h_ref, *, nf):
    f = pl.program_id(1)

    @pl.when(f == 0)
    def _():
        xf = x_ref[...]
        h_ref[...] = (xf * _rms_scale(xf) * g_ref[...]).astype(BF16)
        o_ref[...] = xf

    u = jnp.dot(h_ref[...], wu_ref[...], preferred_element_type=F32)
    a = jnp.square(jnp.maximum(u, 0.0)).astype(BF16)
    o_ref[...] += jnp.dot(a, wd_ref[...], preferred_element_type=F32)

    @pl.when(f == nf - 1)
    def _():
        y = o_ref[...]
        o_ref[...] = y * _rms_scale(y) * gf_ref[...]


def _mlp(x, g_mlp, w_up, w_down, g_final, tm, tf):
    m, d = x.shape
    ff = w_up.shape[1]
    nf = ff // tf
    kernel = functools.partial(_mlp_kernel, nf=nf)
    return pl.pallas_call(
        kernel,
        grid=(m // tm, nf),
        in_specs=[
            pl.BlockSpec((tm, d), lambda i, f: (i, 0)),
            pl.BlockSpec((1, d), lambda i, f: (0, 0)),
            pl.BlockSpec((d, tf), lambda i, f: (0, f)),
            pl.BlockSpec((tf, d), lambda i, f: (f, 0)),
            pl.BlockSpec((1, d), lambda i, f: (0, 0)),
        ],
        out_specs=pl.BlockSpec((tm, d), lambda i, f: (i, 0)),
        out_shape=jax.ShapeDtypeStruct((m, d), F32),
        scratch_shapes=[pltpu.VMEM((tm, d), BF16)],
        compiler_params=_params(("parallel", "arbitrary")),
        name="mlp_final_norm",
    )(x, g_mlp, w_up, w_down, g_final)


def _trunk(x, mem, w, *, tiles):
    b, s, d = x.shape
    m = b * s
    x2 = x.reshape(m, d)
    h = _norm_bf16(x2, w["g_mix"], tiles["norm"])
    z = _matmul(h, w["w_in"], tiles["in_m"], tiles["in_n"])
    tk = min(tiles["gqa_k"], s)
    q_rope, k_rope, vt_ext = _qk_prep(z, _rope_tables(s), w["g_q"], w["g_k"], s, tk)
    ya = _na_attention(z, w["na_bias"], w["na_mask"], b, s)
    yb = _gqa_attention(q_rope, k_rope, vt_ext, b, s, tiles["gqa_q"], tk)
    x2 = _mix(x2, ya, yb, z, w["w_pa"], w["w_pb"], w["w_o"], tiles["mix"])
    kv = _mem_kv(mem.reshape(b * N_MEM, d), w["g_mem"], w["w_ckv"])
    x2 = _cross(x2, kv, w["g_cross"], w["w_cq"], w["w_co"], s, tiles["cross"])
    y = _mlp(x2, w["g_mlp"], w["w_up"], w["w_down"], w["g_final"], tiles["mlp_m"], tiles["mlp_f"])
    return y.reshape(b, s, d)


def _prepare_weights(g_mix, w_in, rpb, g_q, g_k, w_pa, w_pb, w_o, g_cross, g_mem, w_cq, w_ckv, w_co,
                     g_mlp, w_up, w_down, g_final):
    split = 3 * NA_W + GQA_QW
    kv_end = split + 2 * GQA_KVW
    w_in0 = w_in[0]
    w_in_p = jnp.concatenate([w_in0[:, :split], w_in0[:, kv_end:], w_in0[:, split:kv_end]], axis=1)
    na_bias, na_mask = _na_tables(rpb[0])
    return dict(
        g_mix=g_mix[0][None], w_in=w_in_p.astype(BF16), na_bias=na_bias, na_mask=na_mask,
        g_q=g_q[0][None], g_k=g_k[0][None],
        w_pa=w_pa[0].astype(BF16), w_pb=w_pb[0].astype(BF16), w_o=w_o[0].astype(BF16),
        g_cross=g_cross[0][None], g_mem=g_mem[0][None],
        w_cq=w_cq[0].astype(BF16), w_ckv=w_ckv[0].astype(BF16), w_co=w_co[0].astype(BF16),
        g_mlp=g_mlp[0][None], w_up=w_up[0].astype(BF16), w_down=w_down[0].astype(BF16),
        g_final=g_final[None],
    )


TILES = dict(norm=512, in_m=1024, in_n=2176, gqa_q=256, gqa_k=512, mix=512, cross=512,
             mlp_m=512, mlp_f=1024)


def kernel(x_prompt, x_sample, mem_prompt, mem_sample, g_mix, w_in, rpb, g_q, g_k, w_pa, w_pb, w_o, g_cross, g_mem, w_cq, w_ckv, w_co, g_mlp, w_up, w_down, g_final):
    w = _prepare_weights(g_mix, w_in, rpb, g_q, g_k, w_pa, w_pb, w_o, g_cross, g_mem, w_cq, w_ckv, w_co,
                         g_mlp, w_up, w_down, g_final)
    y_prompt = _trunk(x_prompt, mem_prompt, w, tiles=TILES)
    y_sample = _trunk(x_sample, mem_sample, w, tiles=TILES)
    return (y_prompt, y_sample)
```

```python
import functools

import jax
import jax.numpy as jnp
import numpy as np
from jax import lax
from jax.experimental import pallas as pl
from jax.experimental.pallas import tpu as pltpu

D_MODEL = 2048
GRID_W = 64
HEAD_DIM = 128
NA_HEADS = 8
NA_WIN_H = 8
NA_WIN_W = 16
GQA_Q_HEADS = 8
GQA_KV_HEADS = 2
GQA_GROUP = GQA_Q_HEADS // GQA_KV_HEADS
ROPE_THETA = 10000.0
ROPE_AXIS_PAIRS = HEAD_DIM // 4
N_MEM = 256
CROSS_HEADS = 4
D_FF = 4 * D_MODEL
EPS = 1e-6
NEG_INF = -1e30
NA_W = NA_HEADS * HEAD_DIM
GQA_QW = GQA_Q_HEADS * HEAD_DIM
GQA_KVW = GQA_KV_HEADS * HEAD_DIM
CROSS_W = CROSS_HEADS * HEAD_DIM
SCALE = HEAD_DIM ** -0.5
LOG2E = 1.4426950408889634

Z_GQ = 3 * NA_W
Z_GA = Z_GQ + GQA_QW
Z_GB = Z_GA + D_MODEL
Z_GK = Z_GB + D_MODEL
Z_GV = Z_GK + GQA_KVW
D_IN = Z_GV + GQA_KVW

NA_QROWS = 4
NA_KROWS = 12
NA_TQ = NA_QROWS * GRID_W
NA_TK = NA_KROWS * GRID_W

GQA_SUM_ROWS = 16
GQA_VT_ROWS = HEAD_DIM + GQA_SUM_ROWS
GQA_UNROLL = 4
NA_UNROLL = 4

BF16 = jnp.bfloat16
F32 = jnp.float32
VMEM_LIMIT = 56 * 1024 * 1024


def _params(semantics):
    return pltpu.CompilerParams(dimension_semantics=semantics, vmem_limit_bytes=VMEM_LIMIT)


def _rms_scale(xf):
    return lax.rsqrt(jnp.mean(xf * xf, axis=-1, keepdims=True) + EPS)


def _norm_kernel(x_ref, g_ref, o_ref):
    xf = x_ref[...]
    o_ref[...] = (xf * _rms_scale(xf) * g_ref[...]).astype(BF16)


def _norm_bf16(x, g, tm):
    m, d = x.shape
    return pl.pallas_call(
        _norm_kernel,
        grid=(m // tm,),
        in_specs=[pl.BlockSpec((tm, d), lambda i: (i, 0)), pl.BlockSpec((1, d), lambda i: (0, 0))],
        out_specs=pl.BlockSpec((tm, d), lambda i: (i, 0)),
        out_shape=jax.ShapeDtypeStruct((m, d), BF16),
        compiler_params=_params(("parallel",)),
        name="norm_bf16",
    )(x, g)


def _matmul_kernel(a_ref, w_ref, o_ref):
    o_ref[...] = jnp.dot(a_ref[...], w_ref[...], preferred_element_type=F32).astype(o_ref.dtype)


def _matmul(a, w, tm, tn):
    m, k = a.shape
    n = w.shape[1]
    return pl.pallas_call(
        _matmul_kernel,
        grid=(n // tn, m // tm),
        in_specs=[pl.BlockSpec((tm, k), lambda j, i: (i, 0)), pl.BlockSpec((k, tn), lambda j, i: (0, j))],
        out_specs=pl.BlockSpec((tm, tn), lambda j, i: (i, j)),
        out_shape=jax.ShapeDtypeStruct((m, n), BF16),
        compiler_params=_params(("parallel", "parallel")),
        name="in_proj",
    )(a, w)


def _rope_tables(s):
    t = jnp.arange(s)
    row = (t // GRID_W).astype(F32)
    col = (t % GRID_W).astype(F32)
    inv_freq = ROPE_THETA ** (-jnp.arange(ROPE_AXIS_PAIRS, dtype=F32) / ROPE_AXIS_PAIRS)
    ang_r = row[:, None] * inv_freq[None, :]
    ang_c = col[:, None] * inv_freq[None, :]
    cr, sr, cc, sc = jnp.cos(ang_r), jnp.sin(ang_r), jnp.cos(ang_c), jnp.sin(ang_c)
    zero = jnp.zeros_like(sr)
    c = jnp.concatenate([cr, cr, cc, cc], axis=-1)
    s1 = jnp.concatenate([zero, sr, zero, sc], axis=-1)
    s2 = jnp.concatenate([-sr, zero, -sc, zero], axis=-1)
    return c, s1, s2


def _qk_prep_kernel(q_ref, k_ref, v_ref, nav_ref, c_ref, s1_ref, s2_ref, gq_ref, gk_ref,
                    qo_ref, ko_ref, vo_ref, navo_ref):
    c = c_ref[...]
    s1 = s1_ref[...]
    s2 = s2_ref[...]
    half = ROPE_AXIS_PAIRS

    def norm_rope(xh, g):
        xf = xh.astype(F32)
        y = xf * _rms_scale(xf) * g
        return y * c + pltpu.roll(y, half, 1) * s1 + pltpu.roll(y, HEAD_DIM - half, 1) * s2

    gq = gq_ref[...]
    gk = gk_ref[...]
    for h in range(GQA_Q_HEADS):
        sl = slice(h * HEAD_DIM, (h + 1) * HEAD_DIM)
        qo_ref[:, sl] = (norm_rope(q_ref[:, sl], gq) * (SCALE * LOG2E)).astype(BF16)
    for h in range(GQA_KV_HEADS):
        sl = slice(h * HEAD_DIM, (h + 1) * HEAD_DIM)
        ko_ref[:, sl] = norm_rope(k_ref[:, sl], gk).astype(BF16)
        vo_ref[h, :HEAD_DIM, :] = v_ref[:, sl].astype(F32).T.astype(BF16)
        vo_ref[h, HEAD_DIM:, :] = jnp.ones((GQA_SUM_ROWS, vo_ref.shape[-1]), BF16)
    chunk = navo_ref.shape[-1]
    for h in range(NA_HEADS):
        for c in range(navo_ref.shape[1]):
            blk = nav_ref[c * chunk:(c + 1) * chunk, h * HEAD_DIM:(h + 1) * HEAD_DIM]
            navo_ref[h, c] = blk.astype(F32).T.astype(BF16)


def _qk_prep(z, tables, g_q, g_k, s, tm):
    m = z.shape[0]
    nt = s // tm
    tab_spec = pl.BlockSpec((tm, HEAD_DIM), lambda i: (i % nt, 0))
    vec_spec = pl.BlockSpec((1, HEAD_DIM), lambda i: (0, 0))
    row = lambda i: (i, 0)
    na_chunk = (NA_WIN_H // 2) * GRID_W
    return pl.pallas_call(
        _qk_prep_kernel,
        grid=(m // tm,),
        in_specs=[
            pl.BlockSpec((tm, GQA_QW), lambda i: (i, Z_GQ // GQA_QW)),
            pl.BlockSpec((tm, GQA_KVW), lambda i: (i, Z_GK // GQA_KVW)),
            pl.BlockSpec((tm, GQA_KVW), lambda i: (i, Z_GV // GQA_KVW)),
            pl.BlockSpec((tm, NA_W), lambda i: (i, 2)),
            tab_spec, tab_spec, tab_spec, vec_spec, vec_spec,
        ],
        out_specs=[pl.BlockSpec((tm, GQA_QW), row), pl.BlockSpec((tm, GQA_KVW), row),
                   pl.BlockSpec((None, GQA_KV_HEADS, None, GQA_VT_ROWS, tm), lambda i: (i // nt, 0, i % nt, 0, 0)),
                   pl.BlockSpec((None, NA_HEADS, tm // na_chunk, HEAD_DIM, na_chunk),
                                lambda i: (i // nt, 0, i % nt, 0, 0))],
        out_shape=[jax.ShapeDtypeStruct((m, GQA_QW), BF16), jax.ShapeDtypeStruct((m, GQA_KVW), BF16),
                   jax.ShapeDtypeStruct((m // s, GQA_KV_HEADS, nt, GQA_VT_ROWS, tm), BF16),
                   jax.ShapeDtypeStruct((m // s, NA_HEADS, s // na_chunk, HEAD_DIM, na_chunk), BF16)],
        compiler_params=_params(("parallel",)),
        name="qk_prep",
    )(z, z, z, z, *tables, g_q, g_k)


def _gqa_kernel(q_ref, k_ref, vt_ref, o_ref, s_ref, m_ref, acc_ref, *, tk, nk):
    m_ref[...] = jnp.full(m_ref.shape, -jnp.inf, F32)
    acc_ref[...] = jnp.zeros(acc_ref.shape, F32)

    def scores(c, slot):
        off = pl.multiple_of(c * tk, tk)
        k = k_ref[pl.ds(off, tk), :]
        for g in range(GQA_GROUP):
            q = q_ref[:, g * HEAD_DIM:(g + 1) * HEAD_DIM]
            s_ref[slot, g] = lax.dot_general(k, q, (((1,), (1,)), ((), ())), preferred_element_type=F32)

    def update(c, slot):
        vt = vt_ref[c]
        for g in range(GQA_GROUP):
            st = s_ref[slot, g]
            m_prev = m_ref[g]
            m_new = jnp.maximum(m_prev, jnp.max(st, axis=0, keepdims=True))
            alpha = jnp.exp2(m_prev - m_new)
            pt = jnp.exp2(st - m_new).astype(BF16)
            acc_ref[g] = acc_ref[g] * alpha + jnp.dot(vt, pt, preferred_element_type=F32)
            m_ref[g] = m_new

    scores(0, 0)

    def body(i, carry):
        c = GQA_UNROLL * i
        for u in range(GQA_UNROLL):
            scores(jnp.minimum(c + u + 1, nk - 1), (u + 1) % 2)
            update(c + u, u % 2)
        return carry

    lax.fori_loop(0, nk // GQA_UNROLL, body, 0)
    for g in range(GQA_GROUP):
        out_t = acc_ref[g, :HEAD_DIM, :] / acc_ref[g, HEAD_DIM:HEAD_DIM + 1, :]
        o_ref[:, g * HEAD_DIM:(g + 1) * HEAD_DIM] = out_t.T.astype(BF16)


def _gqa_attention(q, k, vt, b, s, tq, tk):
    m = q.shape[0]
    nq = s // tq
    nk = s // tk
    assert nk % GQA_UNROLL == 0 and GQA_UNROLL % 2 == 0
    gw = GQA_GROUP * HEAD_DIM
    kernel = functools.partial(_gqa_kernel, tk=tk, nk=nk)
    return pl.pallas_call(
        kernel,
        grid=(b, GQA_KV_HEADS, nq),
        in_specs=[
            pl.BlockSpec((tq, gw), lambda bi, kh, i: (bi * nq + i, kh)),
            pl.BlockSpec((s, HEAD_DIM), lambda bi, kh, i: (bi, kh)),
            pl.BlockSpec((None, None, nk, GQA_VT_ROWS, tk), lambda bi, kh, i: (bi, kh, 0, 0, 0)),
        ],
        out_specs=pl.BlockSpec((tq, gw), lambda bi, kh, i: (bi * nq + i, kh)),
        out_shape=jax.ShapeDtypeStruct((m, GQA_QW), BF16),
        scratch_shapes=[
            pltpu.VMEM((2, GQA_GROUP, tk, tq), F32),
            pltpu.VMEM((GQA_GROUP, 1, tq), F32),
            pltpu.VMEM((GQA_GROUP, GQA_VT_ROWS, tq), F32),
        ],
        compiler_params=_params(("parallel", "parallel", "parallel")),
        name="gqa_flash",
    )(q, k, vt)


def _na_tables(rpb):
    rows = 4 * NA_KROWS
    nr, nc = 2 * NA_WIN_H - 1, 2 * NA_WIN_W - 1
    cols = np.arange(GRID_W)
    cidx = np.clip(cols[None, :] - cols[:, None] + NA_WIN_W - 1, 0, nc - 1)
    col_start = np.clip(cols - NA_WIN_W // 2, 0, GRID_W - NA_WIN_W)
    col_ok = (cols[None, :] >= col_start[:, None]) & (cols[None, :] < col_start[:, None] + NA_WIN_W)
    sel_c = ((cidx[None] == np.arange(nc)[:, None, None]) & col_ok[None]).astype(np.float32)
    sel_r = np.zeros((3, NA_QROWS, NA_KROWS, nr), np.float32)
    for t, r0 in enumerate((0, 2 * NA_QROWS, rows - NA_QROWS)):
        kb = int(np.clip(r0 - NA_WIN_H // 2, 0, rows - NA_KROWS))
        for i in range(NA_QROWS):
            qr = r0 + i
            start = int(np.clip(qr - NA_WIN_H // 2, 0, rows - NA_WIN_H))
            for j in range(NA_KROWS):
                kr = kb + j
                if start <= kr < start + NA_WIN_H:
                    sel_r[t, i, j, kr - qr + NA_WIN_H - 1] = 1.0
    hp = lax.Precision.HIGHEST
    band = jnp.einsum("hrc,cqk->hrqk", rpb.astype(F32) * LOG2E, jnp.asarray(sel_c), precision=hp)
    bias = jnp.einsum("tijr,hrqk->htjkiq", jnp.asarray(sel_r), band, precision=hp)
    bias = bias.reshape(NA_HEADS, 3, NA_TK, NA_TQ)
    mask = np.einsum("tijr,qk->tjkiq", sel_r, col_ok.astype(np.float32)).reshape(3, NA_TK, NA_TQ)
    return bias, jnp.asarray(mask)


def _na_kernel(q_ref, k_ref, vt_ref, bias_ref, mask_ref, o_ref, s_ref, *, rows, nsub):
    step = pl.program_id(2)
    chunk_rows = NA_WIN_H // 2
    chunk = chunk_rows * GRID_W

    def key_block(j):
        r0 = (step * nsub + j) * NA_QROWS
        return r0, jnp.clip(r0 - NA_WIN_H // 2, 0, rows - NA_KROWS)

    def scores(j, slot):
        _, kb = key_block(j)
        k = k_ref[pl.ds(pl.multiple_of(kb * GRID_W, chunk), NA_TK), :]
        q = q_ref[pl.ds(pl.multiple_of(j * NA_TQ, NA_TQ), NA_TQ), :]
        s_ref[slot] = lax.dot_general(k, q, (((1,), (1,)), ((), ())), preferred_element_type=F32)

    def finish(j, slot):
        r0, kb = key_block(j)
        t = jnp.where(r0 == 0, 0, jnp.where(r0 == rows - NA_QROWS, 2, 1))
        st = jnp.where(mask_ref[t] != 0.0, s_ref[slot] + bias_ref[t], NEG_INF * LOG2E)
        m = jnp.max(st, axis=0, keepdims=True)
        p = jnp.exp2(st - m)
        l = jnp.sum(p, axis=0, keepdims=True)
        pb = p.astype(BF16)
        c0 = kb // chunk_rows
        ot = jnp.dot(vt_ref[c0], pb[:chunk], preferred_element_type=F32)
        for c in range(1, NA_KROWS // chunk_rows):
            ot += jnp.dot(vt_ref[c0 + c], pb[c * chunk:(c + 1) * chunk], preferred_element_type=F32)
        o_ref[pl.ds(pl.multiple_of(j * NA_TQ, NA_TQ), NA_TQ), :] = (ot / l).T.astype(BF16)

    scores(0, 0)

    def body(i, carry):
        j = NA_UNROLL * i
        for u in range(NA_UNROLL):
            scores(jnp.minimum(j + u + 1, nsub - 1), (u + 1) % 2)
            finish(j + u, u % 2)
        return carry

    lax.fori_loop(0, nsub // NA_UNROLL, body, 0)


def _na_attention(z, vt, bias, mask, b, s, nsub):
    m = z.shape[0]
    rows = s // GRID_W
    assert nsub % NA_UNROLL == 0 and NA_UNROLL % 2 == 0
    tq = nsub * NA_TQ
    nstep = s // tq
    nchunk = vt.shape[2]
    kernel = functools.partial(_na_kernel, rows=rows, nsub=nsub)
    return pl.pallas_call(
        kernel,
        grid=(NA_HEADS, b, nstep),
        in_specs=[
            pl.BlockSpec((tq, HEAD_DIM), lambda h, bi, i: (bi * nstep + i, h)),
            pl.BlockSpec((s, HEAD_DIM), lambda h, bi, i: (bi, NA_HEADS + h)),
            pl.BlockSpec((None, None, nchunk, HEAD_DIM, vt.shape[-1]), lambda h, bi, i: (bi, h, 0, 0, 0)),
            pl.BlockSpec((None, 3, NA_TK, NA_TQ), lambda h, bi, i: (h, 0, 0, 0)),
            pl.BlockSpec((3, NA_TK, NA_TQ), lambda h, bi, i: (0, 0, 0)),
        ],
        out_specs=pl.BlockSpec((tq, HEAD_DIM), lambda h, bi, i: (bi * nstep + i, h)),
        out_shape=jax.ShapeDtypeStruct((m, NA_W), BF16),
        scratch_shapes=[pltpu.VMEM((2, NA_TK, NA_TQ), F32)],
        compiler_params=_params(("parallel", "parallel", "parallel")),
        name="na_attn",
    )(z, z, vt, bias, mask)


def _mix_kernel(x_ref, ya_ref, yb_ref, ga_ref, gb_ref, wpa_ref, wpb_ref, wo_ref, o_ref):
    pa = jnp.dot(ya_ref[...], wpa_ref[...], preferred_element_type=F32)
    pb = jnp.dot(yb_ref[...], wpb_ref[...], preferred_element_type=F32)
    mixed = jax.nn.sigmoid(ga_ref[...].astype(F32)) * pa + jax.nn.sigmoid(gb_ref[...].astype(F32)) * pb
    o_ref[...] = x_ref[...] + jnp.dot(mixed.astype(BF16), wo_ref[...], preferred_element_type=F32)


def _mix(x, ya, yb, z, w_pa, w_pb, w_o, tm):
    m, d = x.shape
    row = lambda i: (i, 0)
    const = lambda i: (0, 0)
    resident = dict(pipeline_mode=pl.Buffered(1))
    return pl.pallas_call(
        _mix_kernel,
        grid=(m // tm,),
        in_specs=[
            pl.BlockSpec((tm, d), row),
            pl.BlockSpec((tm, NA_W), row),
            pl.BlockSpec((tm, GQA_QW), row),
            pl.BlockSpec((tm, d), lambda i: (i, Z_GA // D_MODEL)),
            pl.BlockSpec((tm, d), lambda i: (i, Z_GB // D_MODEL)),
            pl.BlockSpec((NA_W, d), const, **resident),
            pl.BlockSpec((GQA_QW, d), const, **resident),
            pl.BlockSpec((d, d), const, **resident),
        ],
        out_specs=pl.BlockSpec((tm, d), row),
        out_shape=jax.ShapeDtypeStruct((m, d), F32),
        compiler_params=_params(("parallel",)),
        name="mix_out_proj",
    )(x, ya, yb, z, z, w_pa, w_pb, w_o)


def _mem_kv_kernel(mem_ref, g_ref, w_ref, o_ref):
    xf = mem_ref[...]
    h = (xf * _rms_scale(xf) * g_ref[...]).astype(BF16)
    o_ref[...] = jnp.dot(h, w_ref[...], preferred_element_type=F32).astype(BF16)


def _mem_kv(mem, g_mem, w_ckv):
    m, d = mem.shape
    n = w_ckv.shape[1]
    return pl.pallas_call(
        _mem_kv_kernel,
        grid=(m // N_MEM,),
        in_specs=[
            pl.BlockSpec((N_MEM, d), lambda i: (i, 0)),
            pl.BlockSpec((1, d), lambda i: (0, 0)),
            pl.BlockSpec((d, n), lambda i: (0, 0)),
        ],
        out_specs=pl.BlockSpec((N_MEM, n), lambda i: (i, 0)),
        out_shape=jax.ShapeDtypeStruct((m, n), BF16),
        compiler_params=_params(("parallel",)),
        name="mem_kv",
    )(mem, g_mem, w_ckv)


def _cross_kernel(x_ref, kv_ref, g_ref, wq_ref, wo_ref, o_ref):
    xf = x_ref[...]
    h = (xf * _rms_scale(xf) * g_ref[...]).astype(BF16)
    q = (jnp.dot(h, wq_ref[...], preferred_element_type=F32) * SCALE).astype(BF16)
    outs = []
    for hd in range(CROSS_HEADS):
        sl = slice(hd * HEAD_DIM, (hd + 1) * HEAD_DIM)
        k = kv_ref[:, sl]
        v = kv_ref[:, CROSS_W + hd * HEAD_DIM:CROSS_W + (hd + 1) * HEAD_DIM]
        s = lax.dot_general(q[:, sl], k, (((1,), (1,)), ((), ())), preferred_element_type=F32)
        p = jnp.exp(s - jnp.max(s, axis=-1, keepdims=True))
        l = jnp.sum(p, axis=-1, keepdims=True)
        outs.append((jnp.dot(p.astype(BF16), v, preferred_element_type=F32) / l).astype(BF16))
    o = jnp.concatenate(outs, axis=-1)
    o_ref[...] = xf + jnp.dot(o, wo_ref[...], preferred_element_type=F32)


def _cross(x, kv, g_cross, w_cq, w_co, s, tm):
    m, d = x.shape
    nt = s // tm
    const = lambda i: (0, 0)
    return pl.pallas_call(
        _cross_kernel,
        grid=(m // tm,),
        in_specs=[
            pl.BlockSpec((tm, d), lambda i: (i, 0)),
            pl.BlockSpec((N_MEM, 2 * CROSS_W), lambda i: (i // nt, 0)),
            pl.BlockSpec((1, d), const),
            pl.BlockSpec((d, CROSS_W), const),
            pl.BlockSpec((CROSS_W, d), const),
        ],
        out_specs=pl.BlockSpec((tm, d), lambda i: (i, 0)),
        out_shape=jax.ShapeDtypeStruct((m, d), F32),
        compiler_params=_params(("parallel",)),
        name="cross_attn",
    )(x, kv, g_cross, w_cq, w_co)


def _mlp_kernel(x_ref, g_ref, wu_ref, wd_ref, gf_ref, o_ref, h_ref, *, nf):
    f = pl.program_id(1)

    @pl.when(f == 0)
    def _():
        xf = x_ref[...]
        h_ref[...] = (xf * _rms_scale(xf) * g_ref[...]).astype(BF16)
        o_ref[...] = xf

    u = jnp.dot(h_ref[...], wu_ref[...], preferred_element_type=F32)
    a = jnp.square(jnp.maximum(u, 0.0)).astype(BF16)
    o_ref[...] += jnp.dot(a, wd_ref[...], preferred_element_type=F32)

    @pl.when(f == nf - 1)
    def _():
        y = o_ref[...]
        o_ref[...] = y * _rms_scale(y) * gf_ref[...]


def _mlp(x, g_mlp, w_up, w_down, g_final, tm, tf):
    m, d = x.shape
    ff = w_up.shape[1]
    nf = ff // tf
    kernel = functools.partial(_mlp_kernel, nf=nf)
    return pl.pallas_call(
        kernel,
        grid=(m // tm, nf),
        in_specs=[
            pl.BlockSpec((tm, d), lambda i, f: (i, 0)),
            pl.BlockSpec((1, d), lambda i, f: (0, 0)),
            pl.BlockSpec((d, tf), lambda i, f: (0, f)),
            pl.BlockSpec((tf, d), lambda i, f: (f, 0)),
            pl.BlockSpec((1, d), lambda i, f: (0, 0)),
        ],
        out_specs=pl.BlockSpec((tm, d), lambda i, f: (i, 0)),
        out_shape=jax.ShapeDtypeStruct((m, d), F32),
        scratch_shapes=[pltpu.VMEM((tm, d), BF16)],
        compiler_params=_params(("parallel", "arbitrary")),
        name="mlp_final_norm",
    )(x, g_mlp, w_up, w_down, g_final)


def _trunk(x, mem, w, *, tiles):
    b, s, d = x.shape
    m = b * s
    x2 = x.reshape(m, d)
    h = _norm_bf16(x2, w["g_mix"], tiles["norm"])
    z = _matmul(h, w["w_in"], tiles["in_m"], tiles["in_n"])
    tk = min(tiles["gqa_k"], s)
    q_rope, k_rope, vt, na_vt = _qk_prep(z, _rope_tables(s), w["g_q"], w["g_k"], s, tk)
    ya = _na_attention(z, na_vt, w["na_bias"], w["na_mask"], b, s, tiles["na_sub"])
    yb = _gqa_attention(q_rope, k_rope, vt, b, s, tiles["gqa_q"], tk)
    x2 = _mix(x2, ya, yb, z, w["w_pa"], w["w_pb"], w["w_o"], tiles["mix"])
    kv = _mem_kv(mem.reshape(b * N_MEM, d), w["g_mem"], w["w_ckv"])
    x2 = _cross(x2, kv, w["g_cross"], w["w_cq"], w["w_co"], s, tiles["cross"])
    y = _mlp(x2, w["g_mlp"], w["w_up"], w["w_down"], w["g_final"], tiles["mlp_m"], tiles["mlp_f"])
    return y.reshape(b, s, d)


def _prepare_weights(g_mix, w_in, rpb, g_q, g_k, w_pa, w_pb, w_o, g_cross, g_mem, w_cq, w_ckv, w_co,
                     g_mlp, w_up, w_down, g_final):
    split = 3 * NA_W + GQA_QW
    kv_end = split + 2 * GQA_KVW
    w_in0 = w_in[0]
    w_in_p = jnp.concatenate([w_in0[:, :NA_W] * (SCALE * LOG2E), w_in0[:, NA_W:split], w_in0[:, kv_end:],
                              w_in0[:, split:kv_end]], axis=1)
    na_bias, na_mask = _na_tables(rpb[0])
    return dict(
        g_mix=g_mix[0][None], w_in=w_in_p.astype(BF16), na_bias=na_bias, na_mask=na_mask,
        g_q=g_q[0][None], g_k=g_k[0][None],
        w_pa=w_pa[0].astype(BF16), w_pb=w_pb[0].astype(BF16), w_o=w_o[0].astype(BF16),
        g_cross=g_cross[0][None], g_mem=g_mem[0][None],
        w_cq=w_cq[0].astype(BF16), w_ckv=w_ckv[0].astype(BF16), w_co=w_co[0].astype(BF16),
        g_mlp=g_mlp[0][None], w_up=w_up[0].astype(BF16), w_down=w_down[0].astype(BF16),
        g_final=g_final[None],
    )


TILES = dict(norm=512, in_m=1024, in_n=2176, gqa_q=256, gqa_k=512, na_sub=16, mix=512, cross=512,
             mlp_m=512, mlp_f=1024)


def kernel(x_prompt, x_sample, mem_prompt, mem_sample, g_mix, w_in, rpb, g_q, g_k, w_pa, w_pb, w_o, g_cross, g_mem, w_cq, w_ckv, w_co, g_mlp, w_up, w_down, g_final):
    w = _prepare_weights(g_mix, w_in, rpb, g_q, g_k, w_pa, w_pb, w_o, g_cross, g_mem, w_cq, w_ckv, w_co,
                         g_mlp, w_up, w_down, g_final)
    y_prompt = _trunk(x_prompt, mem_prompt, w, tiles=TILES)
    y_sample = _trunk(x_sample, mem_sample, w, tiles=TILES)
    return (y_prompt, y_sample)
```

```python
import functools

import jax
import jax.numpy as jnp
import numpy as np
from jax import lax
from jax.experimental import pallas as pl
from jax.experimental.pallas import tpu as pltpu

D_MODEL = 2048
GRID_W = 64
HEAD_DIM = 128
NA_HEADS = 8
NA_WIN_H = 8
NA_WIN_W = 16
GQA_Q_HEADS = 8
GQA_KV_HEADS = 2
GQA_GROUP = GQA_Q_HEADS // GQA_KV_HEADS
ROPE_THETA = 10000.0
ROPE_AXIS_PAIRS = HEAD_DIM // 4
N_MEM = 256
CROSS_HEADS = 4
D_FF = 4 * D_MODEL
EPS = 1e-6
NEG_INF = -1e30
NA_W = NA_HEADS * HEAD_DIM
GQA_QW = GQA_Q_HEADS * HEAD_DIM
GQA_KVW = GQA_KV_HEADS * HEAD_DIM
CROSS_W = CROSS_HEADS * HEAD_DIM
SCALE = HEAD_DIM ** -0.5
LOG2E = 1.4426950408889634

Z_GQ = 3 * NA_W
Z_GA = Z_GQ + GQA_QW
Z_GB = Z_GA + D_MODEL
Z_GK = Z_GB + D_MODEL
Z_GV = Z_GK + GQA_KVW
D_IN = Z_GV + GQA_KVW

NA_QROWS = 4
NA_KROWS = 12
NA_TQ = NA_QROWS * GRID_W
NA_TK = NA_KROWS * GRID_W

GQA_SUM_ROWS = 16
GQA_VT_ROWS = HEAD_DIM + GQA_SUM_ROWS
GQA_UNROLL = 8
NA_UNROLL = 4

BF16 = jnp.bfloat16
F32 = jnp.float32
VMEM_LIMIT = 56 * 1024 * 1024


def _params(semantics):
    return pltpu.CompilerParams(dimension_semantics=semantics, vmem_limit_bytes=VMEM_LIMIT)


def _aligned(offset, multiple):
    return offset if isinstance(offset, int) else pl.multiple_of(offset, multiple)


def _rms_scale(xf):
    return lax.rsqrt(jnp.mean(xf * xf, axis=-1, keepdims=True) + EPS)


def _norm_kernel(x_ref, g_ref, o_ref):
    xf = x_ref[...]
    o_ref[...] = (xf * _rms_scale(xf) * g_ref[...]).astype(BF16)


def _norm_bf16(x, g, tm):
    m, d = x.shape
    return pl.pallas_call(
        _norm_kernel,
        grid=(m // tm,),
        in_specs=[pl.BlockSpec((tm, d), lambda i: (i, 0)), pl.BlockSpec((1, d), lambda i: (0, 0))],
        out_specs=pl.BlockSpec((tm, d), lambda i: (i, 0)),
        out_shape=jax.ShapeDtypeStruct((m, d), BF16),
        compiler_params=_params(("parallel",)),
        name="norm_bf16",
    )(x, g)


def _matmul_kernel(a_ref, w_ref, o_ref):
    o_ref[...] = jnp.dot(a_ref[...], w_ref[...], preferred_element_type=F32).astype(o_ref.dtype)


def _matmul(a, w, tm, tn):
    m, k = a.shape
    n = w.shape[1]
    return pl.pallas_call(
        _matmul_kernel,
        grid=(n // tn, m // tm),
        in_specs=[pl.BlockSpec((tm, k), lambda j, i: (i, 0)),
                  pl.BlockSpec((k, tn), lambda j, i: (0, j), pipeline_mode=pl.Buffered(1))],
        out_specs=pl.BlockSpec((tm, tn), lambda j, i: (i, j)),
        out_shape=jax.ShapeDtypeStruct((m, n), BF16),
        compiler_params=_params(("parallel", "parallel")),
        name="in_proj",
    )(a, w)


def _rope_tables(s):
    t = jnp.arange(s)
    row = (t // GRID_W).astype(F32)
    col = (t % GRID_W).astype(F32)
    inv_freq = ROPE_THETA ** (-jnp.arange(ROPE_AXIS_PAIRS, dtype=F32) / ROPE_AXIS_PAIRS)
    ang_r = row[:, None] * inv_freq[None, :]
    ang_c = col[:, None] * inv_freq[None, :]
    cr, sr, cc, sc = jnp.cos(ang_r), jnp.sin(ang_r), jnp.cos(ang_c), jnp.sin(ang_c)
    zero = jnp.zeros_like(sr)
    c = jnp.concatenate([cr, cr, cc, cc], axis=-1)
    s1 = jnp.concatenate([zero, sr, zero, sc], axis=-1)
    s2 = jnp.concatenate([-sr, zero, -sc, zero], axis=-1)
    return c, s1, s2


def _qk_prep_kernel(q_ref, k_ref, v_ref, nav_ref, c_ref, s1_ref, s2_ref, gq_ref, gk_ref,
                    qo_ref, ko_ref, vo_ref, navo_ref):
    c = c_ref[...]
    s1 = s1_ref[...]
    s2 = s2_ref[...]
    half = ROPE_AXIS_PAIRS

    def norm_rope(xh, g):
        xf = xh.astype(F32)
        y = xf * _rms_scale(xf) * g
        return y * c + pltpu.roll(y, half, 1) * s1 + pltpu.roll(y, HEAD_DIM - half, 1) * s2

    gq = gq_ref[...]
    gk = gk_ref[...]
    for h in range(GQA_Q_HEADS):
        sl = slice(h * HEAD_DIM, (h + 1) * HEAD_DIM)
        qo_ref[:, sl] = (norm_rope(q_ref[:, sl], gq) * (SCALE * LOG2E)).astype(BF16)
    for h in range(GQA_KV_HEADS):
        sl = slice(h * HEAD_DIM, (h + 1) * HEAD_DIM)
        ko_ref[:, sl] = norm_rope(k_ref[:, sl], gk).astype(BF16)
        vo_ref[h, :HEAD_DIM, :] = v_ref[:, sl].astype(F32).T.astype(BF16)
        vo_ref[h, HEAD_DIM:, :] = jnp.ones((GQA_SUM_ROWS, vo_ref.shape[-1]), BF16)
    chunk = navo_ref.shape[-1]
    for h in range(NA_HEADS):
        for c in range(navo_ref.shape[1]):
            blk = nav_ref[c * chunk:(c + 1) * chunk, h * HEAD_DIM:(h + 1) * HEAD_DIM]
            navo_ref[h, c] = blk.astype(F32).T.astype(BF16)


def _qk_prep(z, tables, g_q, g_k, s, tm):
    m = z.shape[0]
    nt = s // tm
    tab_spec = pl.BlockSpec((tm, HEAD_DIM), lambda i: (i % nt, 0))
    vec_spec = pl.BlockSpec((1, HEAD_DIM), lambda i: (0, 0))
    row = lambda i: (i, 0)
    na_chunk = (NA_WIN_H // 2) * GRID_W
    return pl.pallas_call(
        _qk_prep_kernel,
        grid=(m // tm,),
        in_specs=[
            pl.BlockSpec((tm, GQA_QW), lambda i: (i, Z_GQ // GQA_QW)),
            pl.BlockSpec((tm, GQA_KVW), lambda i: (i, Z_GK // GQA_KVW)),
            pl.BlockSpec((tm, GQA_KVW), lambda i: (i, Z_GV // GQA_KVW)),
            pl.BlockSpec((tm, NA_W), lambda i: (i, 2)),
            tab_spec, tab_spec, tab_spec, vec_spec, vec_spec,
        ],
        out_specs=[pl.BlockSpec((tm, GQA_QW), row), pl.BlockSpec((tm, GQA_KVW), row),
                   pl.BlockSpec((None, GQA_KV_HEADS, None, GQA_VT_ROWS, tm), lambda i: (i // nt, 0, i % nt, 0, 0)),
                   pl.BlockSpec((None, NA_HEADS, tm // na_chunk, HEAD_DIM, na_chunk),
                                lambda i: (i // nt, 0, i % nt, 0, 0))],
        out_shape=[jax.ShapeDtypeStruct((m, GQA_QW), BF16), jax.ShapeDtypeStruct((m, GQA_KVW), BF16),
                   jax.ShapeDtypeStruct((m // s, GQA_KV_HEADS, nt, GQA_VT_ROWS, tm), BF16),
                   jax.ShapeDtypeStruct((m // s, NA_HEADS, s // na_chunk, HEAD_DIM, na_chunk), BF16)],
        compiler_params=_params(("parallel",)),
        name="qk_prep",
    )(z, z, z, z, *tables, g_q, g_k)


def _gqa_kernel(q_ref, k_ref, vt_ref, o_ref, s_ref, cmax_ref, m_ref, acc_ref, *, tk, nk):
    m_ref[...] = jnp.full(m_ref.shape, -jnp.inf, F32)
    acc_ref[...] = jnp.zeros(acc_ref.shape, F32)

    def scores(c, slot):
        off = _aligned(c * tk, tk)
        k = k_ref[pl.ds(off, tk), :]
        for g in range(GQA_GROUP):
            q = q_ref[:, g * HEAD_DIM:(g + 1) * HEAD_DIM]
            st = lax.dot_general(k, q, (((1,), (1,)), ((), ())), preferred_element_type=F32)
            s_ref[slot, g] = st
            cmax_ref[slot, g] = jnp.max(st, axis=0, keepdims=True)

    def update(c, slot):
        vt = vt_ref[c]
        for g in range(GQA_GROUP):
            st = s_ref[slot, g]
            m_prev = m_ref[g]
            m_new = jnp.maximum(m_prev, cmax_ref[slot, g])
            alpha = jnp.exp2(m_prev - m_new)
            pt = jnp.exp2(st - m_new).astype(BF16)
            acc_ref[g] = acc_ref[g] * alpha + jnp.dot(vt, pt, preferred_element_type=F32)
            m_ref[g] = m_new

    scores(0, 0)

    def trip(c, last):
        for u in range(GQA_UNROLL):
            if not (last and u == GQA_UNROLL - 1):
                scores(c + u + 1, (u + 1) % 2)
            update(c + u, u % 2)

    def body(i, carry):
        trip(GQA_UNROLL * i, False)
        return carry

    lax.fori_loop(0, nk // GQA_UNROLL - 1, body, 0)
    trip(nk - GQA_UNROLL, True)
    for g in range(GQA_GROUP):
        out_t = acc_ref[g, :HEAD_DIM, :] / acc_ref[g, HEAD_DIM:HEAD_DIM + 1, :]
        o_ref[:, g * HEAD_DIM:(g + 1) * HEAD_DIM] = out_t.T.astype(BF16)


def _gqa_attention(q, k, vt, b, s, tq, tk):
    m = q.shape[0]
    nq = s // tq
    nk = s // tk
    assert nk % GQA_UNROLL == 0 and GQA_UNROLL % 2 == 0
    gw = GQA_GROUP * HEAD_DIM
    kernel = functools.partial(_gqa_kernel, tk=tk, nk=nk)
    return pl.pallas_call(
        kernel,
        grid=(b, GQA_KV_HEADS, nq),
        in_specs=[
            pl.BlockSpec((tq, gw), lambda bi, kh, i: (bi * nq + i, kh)),
            pl.BlockSpec((s, HEAD_DIM), lambda bi, kh, i: (bi, kh)),
            pl.BlockSpec((None, None, nk, GQA_VT_ROWS, tk), lambda bi, kh, i: (bi, kh, 0, 0, 0)),
        ],
        out_specs=pl.BlockSpec((tq, gw), lambda bi, kh, i: (bi * nq + i, kh)),
        out_shape=jax.ShapeDtypeStruct((m, GQA_QW), BF16),
        scratch_shapes=[
            pltpu.VMEM((2, GQA_GROUP, tk, tq), F32),
            pltpu.VMEM((2, GQA_GROUP, 1, tq), F32),
            pltpu.VMEM((GQA_GROUP, 1, tq), F32),
            pltpu.VMEM((GQA_GROUP, GQA_VT_ROWS, tq), F32),
        ],
        compiler_params=_params(("parallel", "parallel", "parallel")),
        name="gqa_flash",
    )(q, k, vt)


def _na_tables(rpb):
    rows = 4 * NA_KROWS
    nr, nc = 2 * NA_WIN_H - 1, 2 * NA_WIN_W - 1
    cols = np.arange(GRID_W)
    cidx = np.clip(cols[None, :] - cols[:, None] + NA_WIN_W - 1, 0, nc - 1)
    col_start = np.clip(cols - NA_WIN_W // 2, 0, GRID_W - NA_WIN_W)
    col_ok = (cols[None, :] >= col_start[:, None]) & (cols[None, :] < col_start[:, None] + NA_WIN_W)
    sel_c = ((cidx[None] == np.arange(nc)[:, None, None]) & col_ok[None]).astype(np.float32)
    sel_r = np.zeros((3, NA_QROWS, NA_KROWS, nr), np.float32)
    for t, r0 in enumerate((0, 2 * NA_QROWS, rows - NA_QROWS)):
        kb = int(np.clip(r0 - NA_WIN_H // 2, 0, rows - NA_KROWS))
        for i in range(NA_QROWS):
            qr = r0 + i
            start = int(np.clip(qr - NA_WIN_H // 2, 0, rows - NA_WIN_H))
            for j in range(NA_KROWS):
                kr = kb + j
                if start <= kr < start + NA_WIN_H:
                    sel_r[t, i, j, kr - qr + NA_WIN_H - 1] = 1.0
    hp = lax.Precision.HIGHEST
    band = jnp.einsum("hrc,cqk->hrqk", rpb.astype(F32) * LOG2E, jnp.asarray(sel_c), precision=hp)
    bias = jnp.einsum("tijr,hrqk->htjkiq", jnp.asarray(sel_r), band, precision=hp)
    bias = bias.reshape(NA_HEADS, 3, NA_TK, NA_TQ)
    mask = np.einsum("tijr,qk->tjkiq", sel_r, col_ok.astype(np.float32)).reshape(3, NA_TK, NA_TQ)
    return bias, jnp.asarray(mask)


def _na_kernel(q_ref, k_ref, vt_ref, bias_ref, mask_ref, o_ref, s_ref, *, rows, nsub):
    step = pl.program_id(2)
    chunk_rows = NA_WIN_H // 2
    chunk = chunk_rows * GRID_W

    def key_block(j):
        r0 = (step * nsub + j) * NA_QROWS
        return r0, jnp.clip(r0 - NA_WIN_H // 2, 0, rows - NA_KROWS)

    def scores(j, slot):
        _, kb = key_block(j)
        k = k_ref[pl.ds(_aligned(kb * GRID_W, chunk), NA_TK), :]
        q = q_ref[pl.ds(_aligned(j * NA_TQ, NA_TQ), NA_TQ), :]
        s_ref[slot] = lax.dot_general(k, q, (((1,), (1,)), ((), ())), preferred_element_type=F32)

    def finish(j, slot):
        r0, kb = key_block(j)
        t = jnp.where(r0 == 0, 0, jnp.where(r0 == rows - NA_QROWS, 2, 1))
        st = jnp.where(mask_ref[t] != 0.0, s_ref[slot] + bias_ref[t], NEG_INF * LOG2E)
        m = jnp.max(st, axis=0, keepdims=True)
        p = jnp.exp2(st - m)
        l = jnp.sum(p, axis=0, keepdims=True)
        pb = p.astype(BF16)
        c0 = kb // chunk_rows
        ot = jnp.dot(vt_ref[c0], pb[:chunk], preferred_element_type=F32)
        for c in range(1, NA_KROWS // chunk_rows):
            ot += jnp.dot(vt_ref[c0 + c], pb[c * chunk:(c + 1) * chunk], preferred_element_type=F32)
        o_ref[pl.ds(_aligned(j * NA_TQ, NA_TQ), NA_TQ), :] = (ot / l).T.astype(BF16)

    scores(0, 0)

    def trip(j, last):
        for u in range(NA_UNROLL):
            if not (last and u == NA_UNROLL - 1):
                scores(j + u + 1, (u + 1) % 2)
            finish(j + u, u % 2)

    def body(i, carry):
        trip(NA_UNROLL * i, False)
        return carry

    lax.fori_loop(0, nsub // NA_UNROLL - 1, body, 0)
    trip(nsub - NA_UNROLL, True)


def _na_attention(z, vt, bias, mask, b, s, nsub):
    m = z.shape[0]
    rows = s // GRID_W
    assert nsub % NA_UNROLL == 0 and NA_UNROLL % 2 == 0
    tq = nsub * NA_TQ
    nstep = s // tq
    nchunk = vt.shape[2]
    kernel = functools.partial(_na_kernel, rows=rows, nsub=nsub)
    return pl.pallas_call(
        kernel,
        grid=(NA_HEADS, b, nstep),
        in_specs=[
            pl.BlockSpec((tq, HEAD_DIM), lambda h, bi, i: (bi * nstep + i, h)),
            pl.BlockSpec((s, HEAD_DIM), lambda h, bi, i: (bi, NA_HEADS + h)),
            pl.BlockSpec((None, None, nchunk, HEAD_DIM, vt.shape[-1]), lambda h, bi, i: (bi, h, 0, 0, 0)),
            pl.BlockSpec((None, 3, NA_TK, NA_TQ), lambda h, bi, i: (h, 0, 0, 0)),
            pl.BlockSpec((3, NA_TK, NA_TQ), lambda h, bi, i: (0, 0, 0)),
        ],
        out_specs=pl.BlockSpec((tq, HEAD_DIM), lambda h, bi, i: (bi * nstep + i, h)),
        out_shape=jax.ShapeDtypeStruct((m, NA_W), BF16),
        scratch_shapes=[pltpu.VMEM((2, NA_TK, NA_TQ), F32)],
        compiler_params=_params(("parallel", "parallel", "parallel")),
        name="na_attn",
    )(z, z, vt, bias, mask)


def _mix_kernel(x_ref, ya_ref, yb_ref, ga_ref, gb_ref, wpa_ref, wpb_ref, wo_ref, o_ref):
    pa = jnp.dot(ya_ref[...], wpa_ref[...], preferred_element_type=F32)
    pb = jnp.dot(yb_ref[...], wpb_ref[...], preferred_element_type=F32)
    mixed = jax.nn.sigmoid(ga_ref[...].astype(F32)) * pa + jax.nn.sigmoid(gb_ref[...].astype(F32)) * pb
    o_ref[...] = x_ref[...] + jnp.dot(mixed.astype(BF16), wo_ref[...], preferred_element_type=F32)


def _mix(x, ya, yb, z, w_pa, w_pb, w_o, tm):
    m, d = x.shape
    row = lambda i: (i, 0)
    const = lambda i: (0, 0)
    resident = dict(pipeline_mode=pl.Buffered(1))
    return pl.pallas_call(
        _mix_kernel,
        grid=(m // tm,),
        in_specs=[
            pl.BlockSpec((tm, d), row),
            pl.BlockSpec((tm, NA_W), row),
            pl.BlockSpec((tm, GQA_QW), row),
            pl.BlockSpec((tm, d), lambda i: (i, Z_GA // D_MODEL)),
            pl.BlockSpec((tm, d), lambda i: (i, Z_GB // D_MODEL)),
            pl.BlockSpec((NA_W, d), const, **resident),
            pl.BlockSpec((GQA_QW, d), const, **resident),
            pl.BlockSpec((d, d), const, **resident),
        ],
        out_specs=pl.BlockSpec((tm, d), row),
        out_shape=jax.ShapeDtypeStruct((m, d), F32),
        compiler_params=_params(("parallel",)),
        name="mix_out_proj",
    )(x, ya, yb, z, z, w_pa, w_pb, w_o)


def _mem_kv_kernel(mem_ref, g_ref, w_ref, o_ref):
    xf = mem_ref[...]
    h = (xf * _rms_scale(xf) * g_ref[...]).astype(BF16)
    o_ref[...] = jnp.dot(h, w_ref[...], preferred_element_type=F32).astype(BF16)


def _mem_kv(mem, g_mem, w_ckv):
    m, d = mem.shape
    n = w_ckv.shape[1]
    return pl.pallas_call(
        _mem_kv_kernel,
        grid=(m // N_MEM,),
        in_specs=[
            pl.BlockSpec((N_MEM, d), lambda i: (i, 0)),
            pl.BlockSpec((1, d), lambda i: (0, 0)),
            pl.BlockSpec((d, n), lambda i: (0, 0)),
        ],
        out_specs=pl.BlockSpec((N_MEM, n), lambda i: (i, 0)),
        out_shape=jax.ShapeDtypeStruct((m, n), BF16),
        compiler_params=_params(("parallel",)),
        name="mem_kv",
    )(mem, g_mem, w_ckv)


def _cross_kernel(x_ref, kv_ref, g_ref, wq_ref, wo_ref, o_ref):
    xf = x_ref[...]
    h = (xf * _rms_scale(xf) * g_ref[...]).astype(BF16)
    q = (jnp.dot(h, wq_ref[...], preferred_element_type=F32) * SCALE).astype(BF16)
    outs = []
    for hd in range(CROSS_HEADS):
        sl = slice(hd * HEAD_DIM, (hd + 1) * HEAD_DIM)
        k = kv_ref[:, sl]
        v = kv_ref[:, CROSS_W + hd * HEAD_DIM:CROSS_W + (hd + 1) * HEAD_DIM]
        s = lax.dot_general(q[:, sl], k, (((1,), (1,)), ((), ())), preferred_element_type=F32)
        p = jnp.exp(s - jnp.max(s, axis=-1, keepdims=True))
        l = jnp.sum(p, axis=-1, keepdims=True)
        outs.append((jnp.dot(p.astype(BF16), v, preferred_element_type=F32) / l).astype(BF16))
    o = jnp.concatenate(outs, axis=-1)
    o_ref[...] = xf + jnp.dot(o, wo_ref[...], preferred_element_type=F32)


def _cross(x, kv, g_cross, w_cq, w_co, s, tm):
    m, d = x.shape
    nt = s // tm
    const = lambda i: (0, 0)
    return pl.pallas_call(
        _cross_kernel,
        grid=(m // tm,),
        in_specs=[
            pl.BlockSpec((tm, d), lambda i: (i, 0)),
            pl.BlockSpec((N_MEM, 2 * CROSS_W), lambda i: (i // nt, 0)),
            pl.BlockSpec((1, d), const),
            pl.BlockSpec((d, CROSS_W), const),
            pl.BlockSpec((CROSS_W, d), const),
        ],
        out_specs=pl.BlockSpec((tm, d), lambda i: (i, 0)),
        out_shape=jax.ShapeDtypeStruct((m, d), F32),
        compiler_params=_params(("parallel",)),
        name="cross_attn",
    )(x, kv, g_cross, w_cq, w_co)


def _mlp_kernel(x_ref, g_ref, wu_ref, wd_ref, gf_ref, o_ref, h_ref, *, nf):
    f = pl.program_id(1)

    @pl.when(f == 0)
    def _():
        xf = x_ref[...]
        h_ref[...] = (xf * _rms_scale(xf) * g_ref[...]).astype(BF16)
        o_ref[...] = xf

    u = jnp.dot(h_ref[...], wu_ref[...], preferred_element_type=F32)
    a = jnp.square(jnp.maximum(u, 0.0)).astype(BF16)
    o_ref[...] += jnp.dot(a, wd_ref[...], preferred_element_type=F32)

    @pl.when(f == nf - 1)
    def _():
        y = o_ref[...]
        o_ref[...] = y * _rms_scale(y) * gf_ref[...]


def _mlp(x, g_mlp, w_up, w_down, g_final, tm, tf):
    m, d = x.shape
    ff = w_up.shape[1]
    nf = ff // tf
    kernel = functools.partial(_mlp_kernel, nf=nf)
    return pl.pallas_call(
        kernel,
        grid=(m // tm, nf),
        in_specs=[
            pl.BlockSpec((tm, d), lambda i, f: (i, 0)),
            pl.BlockSpec((1, d), lambda i, f: (0, 0)),
            pl.BlockSpec((d, tf), lambda i, f: (0, f)),
            pl.BlockSpec((tf, d), lambda i, f: (f, 0)),
            pl.BlockSpec((1, d), lambda i, f: (0, 0)),
        ],
        out_specs=pl.BlockSpec((tm, d), lambda i, f: (i, 0)),
        out_shape=jax.ShapeDtypeStruct((m, d), F32),
        scratch_shapes=[pltpu.VMEM((tm, d), BF16)],
        compiler_params=_params(("parallel", "arbitrary")),
        name="mlp_final_norm",
    )(x, g_mlp, w_up, w_down, g_final)


def _trunk(x, mem, w, *, tiles):
    b, s, d = x.shape
    m = b * s
    x2 = x.reshape(m, d)
    h = _norm_bf16(x2, w["g_mix"], tiles["norm"])
    z = _matmul(h, w["w_in"], tiles["in_m"], tiles["in_n"])
    tk = min(tiles["gqa_k"], s)
    q_rope, k_rope, vt, na_vt = _qk_prep(z, w["rope"], w["g_q"], w["g_k"], s, tk)
    ya = _na_attention(z, na_vt, w["na_bias"], w["na_mask"], b, s, tiles["na_sub"])
    yb = _gqa_attention(q_rope, k_rope, vt, b, s, tiles["gqa_q"], tk)
    x2 = _mix(x2, ya, yb, z, w["w_pa"], w["w_pb"], w["w_o"], tiles["mix"])
    kv = _mem_kv(mem.reshape(b * N_MEM, d), w["g_mem"], w["w_ckv"])
    x2 = _cross(x2, kv, w["g_cross"], w["w_cq"], w["w_co"], s, tiles["cross"])
    y = _mlp(x2, w["g_mlp"], w["w_up"], w["w_down"], w["g_final"], tiles["mlp_m"], tiles["mlp_f"])
    return y.reshape(b, s, d)


def _prepare_weights(g_mix, w_in, rpb, g_q, g_k, w_pa, w_pb, w_o, g_cross, g_mem, w_cq, w_ckv, w_co,
                     g_mlp, w_up, w_down, g_final):
    split = 3 * NA_W + GQA_QW
    kv_end = split + 2 * GQA_KVW
    w_in0 = w_in[0]
    w_in_p = jnp.concatenate([w_in0[:, :NA_W] * (SCALE * LOG2E), w_in0[:, NA_W:split], w_in0[:, kv_end:],
                              w_in0[:, split:kv_end]], axis=1)
    na_bias, na_mask = _na_tables(rpb[0])
    return dict(
        g_mix=g_mix[0][None], w_in=w_in_p.astype(BF16), na_bias=na_bias, na_mask=na_mask,
        g_q=g_q[0][None], g_k=g_k[0][None],
        w_pa=w_pa[0].astype(BF16), w_pb=w_pb[0].astype(BF16), w_o=w_o[0].astype(BF16),
        g_cross=g_cross[0][None], g_mem=g_mem[0][None],
        w_cq=w_cq[0].astype(BF16), w_ckv=w_ckv[0].astype(BF16), w_co=w_co[0].astype(BF16),
        g_mlp=g_mlp[0][None], w_up=w_up[0].astype(BF16), w_down=w_down[0].astype(BF16),
        g_final=g_final[None],
    )


TILES = dict(norm=512, in_m=512, in_n=4352, gqa_q=256, gqa_k=512, na_sub=16, mix=512, cross=512,
             mlp_m=512, mlp_f=1024)


def kernel(x_prompt, x_sample, mem_prompt, mem_sample, g_mix, w_in, rpb, g_q, g_k, w_pa, w_pb, w_o, g_cross, g_mem, w_cq, w_ckv, w_co, g_mlp, w_up, w_down, g_final):
    w = _prepare_weights(g_mix, w_in, rpb, g_q, g_k, w_pa, w_pb, w_o, g_cross, g_mem, w_cq, w_ckv, w_co,
                         g_mlp, w_up, w_down, g_final)
    w["rope"] = _rope_tables(max(x_prompt.shape[1], x_sample.shape[1]))
    y_prompt = _trunk(x_prompt, mem_prompt, w, tiles=TILES)
    y_sample = _trunk(x_sample, mem_sample, w, tiles=TILES)
    return (y_prompt, y_sample)
```

```python
import functools

import jax
import jax.numpy as jnp
import numpy as np
from jax import lax
from jax.experimental import pallas as pl
from jax.experimental.pallas import tpu as pltpu

D_MODEL = 2048
GRID_W = 64
HEAD_DIM = 128
NA_HEADS = 8
NA_WIN_H = 8
NA_WIN_W = 16
GQA_Q_HEADS = 8
GQA_KV_HEADS = 2
GQA_GROUP = GQA_Q_HEADS // GQA_KV_HEADS
ROPE_THETA = 10000.0
ROPE_AXIS_PAIRS = HEAD_DIM // 4
N_MEM = 256
CROSS_HEADS = 4
D_FF = 4 * D_MODEL
EPS = 1e-6
NEG_INF = -1e30
NA_W = NA_HEADS * HEAD_DIM
GQA_QW = GQA_Q_HEADS * HEAD_DIM
GQA_KVW = GQA_KV_HEADS * HEAD_DIM
CROSS_W = CROSS_HEADS * HEAD_DIM
SCALE = HEAD_DIM ** -0.5
LOG2E = 1.4426950408889634

Z_GQ = 3 * NA_W
Z_GA = Z_GQ + GQA_QW
Z_GB = Z_GA + D_MODEL
Z_GK = Z_GB + D_MODEL
Z_GV = Z_GK + GQA_KVW
D_IN = Z_GV + GQA_KVW

NA_QROWS = 4
NA_KROWS = 12
NA_TQ = NA_QROWS * GRID_W
NA_TK = NA_KROWS * GRID_W

GQA_SUM_ROWS = 16
GQA_VT_ROWS = HEAD_DIM + GQA_SUM_ROWS
GQA_UNROLL = 8
GQA_MIN_TRIPS = 4
NA_UNROLL = 4

BF16 = jnp.bfloat16
F32 = jnp.float32
VMEM_LIMIT = 56 * 1024 * 1024


def _params(semantics):
    return pltpu.CompilerParams(dimension_semantics=semantics, vmem_limit_bytes=VMEM_LIMIT)


def _aligned(offset, multiple):
    return offset if isinstance(offset, int) else pl.multiple_of(offset, multiple)


def _rms_scale(xf):
    return lax.rsqrt(jnp.mean(xf * xf, axis=-1, keepdims=True) + EPS)


def _norm_kernel(x_ref, g_ref, o_ref):
    xf = x_ref[...]
    o_ref[...] = (xf * _rms_scale(xf) * g_ref[...]).astype(BF16)


def _norm_bf16(x, g, tm):
    m, d = x.shape
    return pl.pallas_call(
        _norm_kernel,
        grid=(m // tm,),
        in_specs=[pl.BlockSpec((tm, d), lambda i: (i, 0)), pl.BlockSpec((1, d), lambda i: (0, 0))],
        out_specs=pl.BlockSpec((tm, d), lambda i: (i, 0)),
        out_shape=jax.ShapeDtypeStruct((m, d), BF16),
        compiler_params=_params(("parallel",)),
        name="norm_bf16",
    )(x, g)


def _matmul_kernel(a_ref, w_ref, o_ref):
    o_ref[...] = jnp.dot(a_ref[...], w_ref[...], preferred_element_type=F32).astype(o_ref.dtype)


def _matmul(a, w, tm, tn):
    m, k = a.shape
    n = w.shape[1]
    return pl.pallas_call(
        _matmul_kernel,
        grid=(n // tn, m // tm),
        in_specs=[pl.BlockSpec((tm, k), lambda j, i: (i, 0)),
                  pl.BlockSpec((k, tn), lambda j, i: (0, j), pipeline_mode=pl.Buffered(1))],
        out_specs=pl.BlockSpec((tm, tn), lambda j, i: (i, j)),
        out_shape=jax.ShapeDtypeStruct((m, n), BF16),
        compiler_params=_params(("parallel", "parallel")),
        name="in_proj",
    )(a, w)


def _rope_head_layout(w):
    lead = w.shape[:-1]
    w4 = w.reshape(*lead, -1, 4, ROPE_AXIS_PAIRS)
    w4 = jnp.stack([w4[..., 0, :], w4[..., 2, :], w4[..., 1, :], w4[..., 3, :]], axis=-2)
    return w4.reshape(*lead, -1)


def _rope_tables(s):
    t = jnp.arange(s)
    row = (t // GRID_W).astype(F32)
    col = (t % GRID_W).astype(F32)
    inv_freq = ROPE_THETA ** (-jnp.arange(ROPE_AXIS_PAIRS, dtype=F32) / ROPE_AXIS_PAIRS)
    ang_r = row[:, None] * inv_freq[None, :]
    ang_c = col[:, None] * inv_freq[None, :]
    cr, sr, cc, sc = jnp.cos(ang_r), jnp.sin(ang_r), jnp.cos(ang_c), jnp.sin(ang_c)
    c = jnp.concatenate([cr, cc, cr, cc], axis=-1)
    sn = jnp.concatenate([-sr, -sc, sr, sc], axis=-1)
    return c, sn


def _qk_prep_kernel(q_ref, k_ref, v_ref, c_ref, sn_ref, gq_ref, gk_ref, qo_ref, ko_ref, vo_ref):
    c = c_ref[...]
    sn = sn_ref[...]

    def norm_rope(xh, g):
        xf = xh.astype(F32)
        y = xf * _rms_scale(xf) * g
        return y * c + pltpu.roll(y, HEAD_DIM // 2, 1) * sn

    gq = gq_ref[...]
    gk = gk_ref[...]
    for h in range(GQA_Q_HEADS):
        sl = slice(h * HEAD_DIM, (h + 1) * HEAD_DIM)
        qo_ref[:, sl] = (norm_rope(q_ref[:, sl], gq) * (SCALE * LOG2E)).astype(BF16)
    for h in range(GQA_KV_HEADS):
        sl = slice(h * HEAD_DIM, (h + 1) * HEAD_DIM)
        ko_ref[:, sl] = norm_rope(k_ref[:, sl], gk).astype(BF16)
        vo_ref[h, :HEAD_DIM, :] = v_ref[:, sl].astype(F32).T.astype(BF16)
        vo_ref[h, HEAD_DIM:, :] = jnp.ones((GQA_SUM_ROWS, vo_ref.shape[-1]), BF16)


def _qk_prep(z, tables, g_q, g_k, s, tm):
    m = z.shape[0]
    nt = s // tm
    tab_spec = pl.BlockSpec((tm, HEAD_DIM), lambda i: (i % nt, 0))
    vec_spec = pl.BlockSpec((1, HEAD_DIM), lambda i: (0, 0))
    row = lambda i: (i, 0)
    return pl.pallas_call(
        _qk_prep_kernel,
        grid=(m // tm,),
        in_specs=[
            pl.BlockSpec((tm, GQA_QW), lambda i: (i, Z_GQ // GQA_QW)),
            pl.BlockSpec((tm, GQA_KVW), lambda i: (i, Z_GK // GQA_KVW)),
            pl.BlockSpec((tm, GQA_KVW), lambda i: (i, Z_GV // GQA_KVW)),
            tab_spec, tab_spec, vec_spec, vec_spec,
        ],
        out_specs=[pl.BlockSpec((tm, GQA_QW), row), pl.BlockSpec((tm, GQA_KVW), row),
                   pl.BlockSpec((None, GQA_KV_HEADS, None, GQA_VT_ROWS, tm), lambda i: (i // nt, 0, i % nt, 0, 0))],
        out_shape=[jax.ShapeDtypeStruct((m, GQA_QW), BF16), jax.ShapeDtypeStruct((m, GQA_KVW), BF16),
                   jax.ShapeDtypeStruct((m // s, GQA_KV_HEADS, nt, GQA_VT_ROWS, tm), BF16)],
        compiler_params=_params(("parallel",)),
        name="qk_prep",
    )(z, z, z, *tables, g_q, g_k)


def _gqa_kernel(q_ref, k_ref, vt_ref, o_ref, s_ref, cmax_ref, m_ref, acc_ref, *, tk, nk, unroll):
    m_ref[...] = jnp.full(m_ref.shape, -jnp.inf, F32)
    acc_ref[...] = jnp.zeros(acc_ref.shape, F32)

    def scores(c, slot):
        off = _aligned(c * tk, tk)
        k = k_ref[pl.ds(off, tk), :]
        for g in range(GQA_GROUP):
            q = q_ref[:, g * HEAD_DIM:(g + 1) * HEAD_DIM]
            st = lax.dot_general(k, q, (((1,), (1,)), ((), ())), preferred_element_type=F32)
            s_ref[slot, g] = st
            cmax_ref[slot, g] = jnp.max(st, axis=0, keepdims=True)

    def update(c, slot):
        vt = vt_ref[c]
        for g in range(GQA_GROUP):
            st = s_ref[slot, g]
            m_prev = m_ref[g]
            m_new = jnp.maximum(m_prev, cmax_ref[slot, g])
            alpha = jnp.exp2(m_prev - m_new)
            pt = jnp.exp2(st - m_new).astype(BF16)
            acc_ref[g] = acc_ref[g] * alpha + jnp.dot(vt, pt, preferred_element_type=F32)
            m_ref[g] = m_new

    scores(0, 0)

    def trip(c, last):
        for u in range(unroll):
            if not (last and u == unroll - 1):
                scores(c + u + 1, (u + 1) % 2)
            update(c + u, u % 2)

    def body(i, carry):
        trip(unroll * i, False)
        return carry

    lax.fori_loop(0, nk // unroll - 1, body, 0)
    trip(nk - unroll, True)
    for g in range(GQA_GROUP):
        out_t = acc_ref[g, :HEAD_DIM, :] / acc_ref[g, HEAD_DIM:HEAD_DIM + 1, :]
        o_ref[:, g * HEAD_DIM:(g + 1) * HEAD_DIM] = out_t.T.astype(BF16)


def _gqa_attention(q, k, vt, b, s, tq, tk):
    m = q.shape[0]
    nq = s // tq
    nk = s // tk
    unroll = max(2, min(GQA_UNROLL, nk // GQA_MIN_TRIPS))
    assert nk % unroll == 0 and unroll % 2 == 0
    gw = GQA_GROUP * HEAD_DIM
    kernel = functools.partial(_gqa_kernel, tk=tk, nk=nk, unroll=unroll)
    return pl.pallas_call(
        kernel,
        grid=(b, GQA_KV_HEADS, nq),
        in_specs=[
            pl.BlockSpec((tq, gw), lambda bi, kh, i: (bi * nq + i, kh)),
            pl.BlockSpec((s, HEAD_DIM), lambda bi, kh, i: (bi, kh)),
            pl.BlockSpec((None, None, nk, GQA_VT_ROWS, tk), lambda bi, kh, i: (bi, kh, 0, 0, 0)),
        ],
        out_specs=pl.BlockSpec((tq, gw), lambda bi, kh, i: (bi * nq + i, kh)),
        out_shape=jax.ShapeDtypeStruct((m, GQA_QW), BF16),
        scratch_shapes=[
            pltpu.VMEM((2, GQA_GROUP, tk, tq), F32),
            pltpu.VMEM((2, GQA_GROUP, 1, tq), F32),
            pltpu.VMEM((GQA_GROUP, 1, tq), F32),
            pltpu.VMEM((GQA_GROUP, GQA_VT_ROWS, tq), F32),
        ],
        compiler_params=_params(("parallel", "parallel", "parallel")),
        name="gqa_flash",
    )(q, k, vt)


def _na_tables(rpb):
    rows = 4 * NA_KROWS
    nr, nc = 2 * NA_WIN_H - 1, 2 * NA_WIN_W - 1
    cols = np.arange(GRID_W)
    cidx = np.clip(cols[None, :] - cols[:, None] + NA_WIN_W - 1, 0, nc - 1)
    col_start = np.clip(cols - NA_WIN_W // 2, 0, GRID_W - NA_WIN_W)
    col_ok = (cols[None, :] >= col_start[:, None]) & (cols[None, :] < col_start[:, None] + NA_WIN_W)
    sel_c = ((cidx[None] == np.arange(nc)[:, None, None]) & col_ok[None]).astype(np.float32)
    sel_r = np.zeros((3, NA_QROWS, NA_KROWS, nr), np.float32)
    for t, r0 in enumerate((0, 2 * NA_QROWS, rows - NA_QROWS)):
        kb = int(np.clip(r0 - NA_WIN_H // 2, 0, rows - NA_KROWS))
        for i in range(NA_QROWS):
            qr = r0 + i
            start = int(np.clip(qr - NA_WIN_H // 2, 0, rows - NA_WIN_H))
            for j in range(NA_KROWS):
                kr = kb + j
                if start <= kr < start + NA_WIN_H:
                    sel_r[t, i, j, kr - qr + NA_WIN_H - 1] = 1.0
    hp = lax.Precision.HIGHEST
    band = jnp.einsum("hrc,cqk->hrqk", rpb.astype(F32) * LOG2E, jnp.asarray(sel_c), precision=hp)
    bias = jnp.einsum("tijr,hrqk->htjkiq", jnp.asarray(sel_r), band, precision=hp)
    bias = bias.reshape(NA_HEADS, 3, NA_TK, NA_TQ)
    mask = np.einsum("tijr,qk->tjkiq", sel_r, col_ok.astype(np.float32)).reshape(3, NA_TK, NA_TQ)
    return bias, jnp.asarray(mask)


def _na_kernel(q_ref, k_ref, v_ref, bias_ref, mask_ref, o_ref, s_ref, *, rows, nsub):
    step = pl.program_id(2)
    chunk = (NA_WIN_H // 2) * GRID_W

    def key_block(j):
        r0 = (step * nsub + j) * NA_QROWS
        return r0, jnp.clip(r0 - NA_WIN_H // 2, 0, rows - NA_KROWS)

    def scores(j, slot):
        _, kb = key_block(j)
        k = k_ref[pl.ds(_aligned(kb * GRID_W, chunk), NA_TK), :]
        q = q_ref[pl.ds(_aligned(j * NA_TQ, NA_TQ), NA_TQ), :]
        s_ref[slot] = lax.dot_general(k, q, (((1,), (1,)), ((), ())), preferred_element_type=F32)

    def finish(j, slot):
        r0, kb = key_block(j)
        t = jnp.where(r0 == 0, 0, jnp.where(r0 == rows - NA_QROWS, 2, 1))
        st = jnp.where(mask_ref[t] != 0.0, s_ref[slot] + bias_ref[t], NEG_INF * LOG2E)
        m = jnp.max(st, axis=0, keepdims=True)
        p = jnp.exp2(st - m)
        l = jnp.sum(p, axis=0, keepdims=True)
        pb = p.astype(BF16)
        v = v_ref[pl.ds(_aligned(kb * GRID_W, chunk), NA_TK), :]
        ot = lax.dot_general(v, pb, (((0,), (0,)), ((), ())), preferred_element_type=F32)
        o_ref[pl.ds(_aligned(j * NA_TQ, NA_TQ), NA_TQ), :] = (ot / l).T.astype(BF16)

    scores(0, 0)

    def trip(j, last):
        for u in range(NA_UNROLL):
            if not (last and u == NA_UNROLL - 1):
                scores(j + u + 1, (u + 1) % 2)
            finish(j + u, u % 2)

    def body(i, carry):
        trip(NA_UNROLL * i, False)
        return carry

    lax.fori_loop(0, nsub // NA_UNROLL - 1, body, 0)
    trip(nsub - NA_UNROLL, True)


def _na_attention(z, bias, mask, b, s, nsub):
    m = z.shape[0]
    rows = s // GRID_W
    assert nsub % NA_UNROLL == 0 and NA_UNROLL % 2 == 0
    tq = nsub * NA_TQ
    nstep = s // tq
    kernel = functools.partial(_na_kernel, rows=rows, nsub=nsub)
    return pl.pallas_call(
        kernel,
        grid=(NA_HEADS, b, nstep),
        in_specs=[
            pl.BlockSpec((tq, HEAD_DIM), lambda h, bi, i: (bi * nstep + i, h)),
            pl.BlockSpec((s, HEAD_DIM), lambda h, bi, i: (bi, NA_HEADS + h)),
            pl.BlockSpec((s, HEAD_DIM), lambda h, bi, i: (bi, 2 * NA_HEADS + h)),
            pl.BlockSpec((None, 3, NA_TK, NA_TQ), lambda h, bi, i: (h, 0, 0, 0)),
            pl.BlockSpec((3, NA_TK, NA_TQ), lambda h, bi, i: (0, 0, 0)),
        ],
        out_specs=pl.BlockSpec((tq, HEAD_DIM), lambda h, bi, i: (bi * nstep + i, h)),
        out_shape=jax.ShapeDtypeStruct((m, NA_W), BF16),
        scratch_shapes=[pltpu.VMEM((2, NA_TK, NA_TQ), F32)],
        compiler_params=_params(("parallel", "parallel", "parallel")),
        name="na_attn",
    )(z, z, z, bias, mask)


def _mix_kernel(x_ref, ya_ref, yb_ref, ga_ref, gb_ref, wpa_ref, wpb_ref, wo_ref, o_ref):
    pa = jnp.dot(ya_ref[...], wpa_ref[...], preferred_element_type=F32)
    pb = jnp.dot(yb_ref[...], wpb_ref[...], preferred_element_type=F32)
    mixed = jax.nn.sigmoid(ga_ref[...].astype(F32)) * pa + jax.nn.sigmoid(gb_ref[...].astype(F32)) * pb
    o_ref[...] = x_ref[...] + jnp.dot(mixed.astype(BF16), wo_ref[...], preferred_element_type=F32)


def _mix(x, ya, yb, z, w_pa, w_pb, w_o, tm):
    m, d = x.shape
    row = lambda i: (i, 0)
    const = lambda i: (0, 0)
    resident = dict(pipeline_mode=pl.Buffered(1))
    return pl.pallas_call(
        _mix_kernel,
        grid=(m // tm,),
        in_specs=[
            pl.BlockSpec((tm, d), row),
            pl.BlockSpec((tm, NA_W), row),
            pl.BlockSpec((tm, GQA_QW), row),
            pl.BlockSpec((tm, d), lambda i: (i, Z_GA // D_MODEL)),
            pl.BlockSpec((tm, d), lambda i: (i, Z_GB // D_MODEL)),
            pl.BlockSpec((NA_W, d), const, **resident),
            pl.BlockSpec((GQA_QW, d), const, **resident),
            pl.BlockSpec((d, d), const, **resident),
        ],
        out_specs=pl.BlockSpec((tm, d), row),
        out_shape=jax.ShapeDtypeStruct((m, d), F32),
        compiler_params=_params(("parallel",)),
        name="mix_out_proj",
    )(x, ya, yb, z, z, w_pa, w_pb, w_o)


def _mem_kv_kernel(mem_ref, g_ref, w_ref, o_ref):
    xf = mem_ref[...]
    h = (xf * _rms_scale(xf) * g_ref[...]).astype(BF16)
    o_ref[...] = jnp.dot(h, w_ref[...], preferred_element_type=F32).astype(BF16)


def _mem_kv(mem, g_mem, w_ckv):
    m, d = mem.shape
    n = w_ckv.shape[1]
    return pl.pallas_call(
        _mem_kv_kernel,
        grid=(m // N_MEM,),
        in_specs=[
            pl.BlockSpec((N_MEM, d), lambda i: (i, 0)),
            pl.BlockSpec((1, d), lambda i: (0, 0)),
            pl.BlockSpec((d, n), lambda i: (0, 0)),
        ],
        out_specs=pl.BlockSpec((N_MEM, n), lambda i: (i, 0)),
        out_shape=jax.ShapeDtypeStruct((m, n), BF16),
        compiler_params=_params(("parallel",)),
        name="mem_kv",
    )(mem, g_mem, w_ckv)


def _cross_kernel(x_ref, kv_ref, g_ref, wq_ref, wo_ref, o_ref):
    xf = x_ref[...]
    h = (xf * _rms_scale(xf) * g_ref[...]).astype(BF16)
    q = (jnp.dot(h, wq_ref[...], preferred_element_type=F32) * SCALE).astype(BF16)
    outs = []
    for hd in range(CROSS_HEADS):
        sl = slice(hd * HEAD_DIM, (hd + 1) * HEAD_DIM)
        k = kv_ref[:, sl]
        v = kv_ref[:, CROSS_W + hd * HEAD_DIM:CROSS_W + (hd + 1) * HEAD_DIM]
        s = lax.dot_general(q[:, sl], k, (((1,), (1,)), ((), ())), preferred_element_type=F32)
        p = jnp.exp(s - jnp.max(s, axis=-1, keepdims=True))
        l = jnp.sum(p, axis=-1, keepdims=True)
        outs.append((jnp.dot(p.astype(BF16), v, preferred_element_type=F32) / l).astype(BF16))
    o = jnp.concatenate(outs, axis=-1)
    o_ref[...] = xf + jnp.dot(o, wo_ref[...], preferred_element_type=F32)


def _cross(x, kv, g_cross, w_cq, w_co, s, tm):
    m, d = x.shape
    nt = s // tm
    const = lambda i: (0, 0)
    return pl.pallas_call(
        _cross_kernel,
        grid=(m // tm,),
        in_specs=[
            pl.BlockSpec((tm, d), lambda i: (i, 0)),
            pl.BlockSpec((N_MEM, 2 * CROSS_W), lambda i: (i // nt, 0)),
            pl.BlockSpec((1, d), const),
            pl.BlockSpec((d, CROSS_W), const),
            pl.BlockSpec((CROSS_W, d), const),
        ],
        out_specs=pl.BlockSpec((tm, d), lambda i: (i, 0)),
        out_shape=jax.ShapeDtypeStruct((m, d), F32),
        compiler_params=_params(("parallel",)),
        name="cross_attn",
    )(x, kv, g_cross, w_cq, w_co)


def _mlp_kernel(x_ref, g_ref, wu_ref, wd_ref, gf_ref, o_ref, h_ref, *, nf):
    f = pl.program_id(1)

    @pl.when(f == 0)
    def _():
        xf = x_ref[...]
        h_ref[...] = (xf * _rms_scale(xf) * g_ref[...]).astype(BF16)
        o_ref[...] = xf

    u = jnp.dot(h_ref[...], wu_ref[...], preferred_element_type=F32)
    a = jnp.square(jnp.maximum(u, 0.0)).astype(BF16)
    o_ref[...] += jnp.dot(a, wd_ref[...], preferred_element_type=F32)

    @pl.when(f == nf - 1)
    def _():
        y = o_ref[...]
        o_ref[...] = y * _rms_scale(y) * gf_ref[...]


def _mlp(x, g_mlp, w_up, w_down, g_final, tm, tf):
    m, d = x.shape
    ff = w_up.shape[1]
    nf = ff // tf
    kernel = functools.partial(_mlp_kernel, nf=nf)
    return pl.pallas_call(
        kernel,
        grid=(m // tm, nf),
        in_specs=[
            pl.BlockSpec((tm, d), lambda i, f: (i, 0)),
            pl.BlockSpec((1, d), lambda i, f: (0, 0)),
            pl.BlockSpec((d, tf), lambda i, f: (0, f)),
            pl.BlockSpec((tf, d), lambda i, f: (f, 0)),
            pl.BlockSpec((1, d), lambda i, f: (0, 0)),
        ],
        out_specs=pl.BlockSpec((tm, d), lambda i, f: (i, 0)),
        out_shape=jax.ShapeDtypeStruct((m, d), F32),
        scratch_shapes=[pltpu.VMEM((tm, d), BF16)],
        compiler_params=_params(("parallel", "arbitrary")),
        name="mlp_final_norm",
    )(x, g_mlp, w_up, w_down, g_final)


def _trunk(x, mem, w, *, tiles):
    b, s, d = x.shape
    m = b * s
    x2 = x.reshape(m, d)
    h = _norm_bf16(x2, w["g_mix"], tiles["norm"])
    z = _matmul(h, w["w_in"], tiles["in_m"], tiles["in_n"])
    tk = min(tiles["gqa_k"], s)
    q_rope, k_rope, vt = _qk_prep(z, w["rope"], w["g_q"], w["g_k"], s, tk)
    ya = _na_attention(z, w["na_bias"], w["na_mask"], b, s, tiles["na_sub"])
    yb = _gqa_attention(q_rope, k_rope, vt, b, s, tiles["gqa_q"], tk)
    x2 = _mix(x2, ya, yb, z, w["w_pa"], w["w_pb"], w["w_o"], tiles["mix"])
    kv = _mem_kv(mem.reshape(b * N_MEM, d), w["g_mem"], w["w_ckv"])
    x2 = _cross(x2, kv, w["g_cross"], w["w_cq"], w["w_co"], s, tiles["cross"])
    y = _mlp(x2, w["g_mlp"], w["w_up"], w["w_down"], w["g_final"], tiles["mlp_m"], tiles["mlp_f"])
    return y.reshape(b, s, d)


def _prepare_weights(g_mix, w_in, rpb, g_q, g_k, w_pa, w_pb, w_o, g_cross, g_mem, w_cq, w_ckv, w_co,
                     g_mlp, w_up, w_down, g_final):
    split = 3 * NA_W + GQA_QW
    kv_end = split + 2 * GQA_KVW
    w_in0 = w_in[0]
    gk_end = split + GQA_KVW
    w_in_p = jnp.concatenate([w_in0[:, :NA_W] * (SCALE * LOG2E), w_in0[:, NA_W:Z_GQ],
                              _rope_head_layout(w_in0[:, Z_GQ:split]), w_in0[:, kv_end:],
                              _rope_head_layout(w_in0[:, split:gk_end]), w_in0[:, gk_end:kv_end]], axis=1)
    na_bias, na_mask = _na_tables(rpb[0])
    return dict(
        g_mix=g_mix[0][None], w_in=w_in_p.astype(BF16), na_bias=na_bias, na_mask=na_mask,
        g_q=_rope_head_layout(g_q[0][None]), g_k=_rope_head_layout(g_k[0][None]),
        w_pa=w_pa[0].astype(BF16), w_pb=w_pb[0].astype(BF16), w_o=w_o[0].astype(BF16),
        g_cross=g_cross[0][None], g_mem=g_mem[0][None],
        w_cq=w_cq[0].astype(BF16), w_ckv=w_ckv[0].astype(BF16), w_co=w_co[0].astype(BF16),
        g_mlp=g_mlp[0][None], w_up=w_up[0].astype(BF16), w_down=w_down[0].astype(BF16),
        g_final=g_final[None],
    )


TILES = dict(norm=512, in_m=512, in_n=4352, gqa_q=256, gqa_k=512, na_sub=16, mix=512, cross=512,
             mlp_m=512, mlp_f=1024)


def kernel(x_prompt, x_sample, mem_prompt, mem_sample, g_mix, w_in, rpb, g_q, g_k, w_pa, w_pb, w_o, g_cross, g_mem, w_cq, w_ckv, w_co, g_mlp, w_up, w_down, g_final):
    w = _prepare_weights(g_mix, w_in, rpb, g_q, g_k, w_pa, w_pb, w_o, g_cross, g_mem, w_cq, w_ckv, w_co,
                         g_mlp, w_up, w_down, g_final)
    w["rope"] = _rope_tables(max(x_prompt.shape[1], x_sample.shape[1]))
    y_prompt = _trunk(x_prompt, mem_prompt, w, tiles=TILES)
    y_sample = _trunk(x_sample, mem_sample, w, tiles=TILES)
    return (y_prompt, y_sample)
```
